```python
import jax
import jax.numpy as jnp
from jax import lax
import numpy as np

D_MODEL = 1024
BATCH = 1
SEQ = 16384
DEPTH = 2
DEC_BATCH = 128
DEC_SEQ = 1
PAST_LEN = 16384
PAGE_SIZE = 128

HEAD_DIM = 64
N_A_LAYERS = DEPTH // 2
N_B_LAYERS = DEPTH - N_A_LAYERS
A_HEADS = D_MODEL // HEAD_DIM
A_KV_HEADS = 4
A_GROUP = A_HEADS // A_KV_HEADS
A_WIDTH = A_HEADS * HEAD_DIM
A_KV_WIDTH = A_KV_HEADS * HEAD_DIM
MOBA_BLOCK = 256
MOBA_TOPK = 3
Q_CHUNK = 128
B_HEADS = D_MODEL // HEAD_DIM
B_KV_HEADS = 2
B_GROUP = B_HEADS // B_KV_HEADS
B_WIDTH = B_HEADS * HEAD_DIM
B_KV_WIDTH = B_KV_HEADS * HEAD_DIM
WINDOW = 128
ROPE_THETA = 10000.0
EPS = 1e-6
NEG_INF = -1e30
SCALE = HEAD_DIM ** -0.5

kernel_name = 'yoco_moba_swa_sink_decode_step'

F32 = jnp.float32


def rmsnorm(x, g):
    xf = x.astype(F32)
    y = xf * lax.rsqrt(jnp.mean(xf * xf, axis=-1, keepdims=True) + EPS)
    return (y * g.astype(F32)).astype(x.dtype)


def rope(x, pos):
    half = HEAD_DIM // 2
    inv = 1.0 / (ROPE_THETA ** (jnp.arange(half, dtype=F32) / half))
    ang = pos.astype(F32)[:, None] * inv[None, :]
    c = jnp.cos(ang)[:, None, :]
    s = jnp.sin(ang)[:, None, :]
    xf = x.astype(F32)
    x1, x2 = xf[..., :half], xf[..., half:]
    return jnp.concatenate([x1 * c - x2 * s, x2 * c + x1 * s], axis=-1).astype(x.dtype)


def gated_out(o, gate, w_out):
    n, t = o.shape[:2]
    return (o.reshape(n, t, -1) * jax.nn.silu(gate)) @ w_out


def proj_a(x, pos, g, w_in, qn, kn):
    n, t, _ = x.shape
    z = rmsnorm(x, g) @ w_in
    q = z[..., :A_WIDTH].reshape(n, t, A_HEADS, HEAD_DIM)
    k = z[..., A_WIDTH:A_WIDTH + A_KV_WIDTH].reshape(n, t, A_KV_HEADS, HEAD_DIM)
    v = z[..., A_WIDTH + A_KV_WIDTH:A_WIDTH + 2 * A_KV_WIDTH].reshape(n, t, A_KV_HEADS, HEAD_DIM)
    gate = z[..., A_WIDTH + 2 * A_KV_WIDTH:]
    return rope(rmsnorm(q, qn), pos), rope(rmsnorm(k, kn), pos), v, gate


def to_blocks(k):
    L = k.shape[0]
    nblk = -(-L // MOBA_BLOCK)
    k = jnp.pad(k, ((0, nblk * MOBA_BLOCK - L), (0, 0), (0, 0)))
    return k.reshape(nblk, MOBA_BLOCK, A_KV_HEADS, HEAD_DIM).transpose(2, 0, 1, 3)


def block_means(kblk):
    return kblk.astype(F32).sum(axis=2) / MOBA_BLOCK


def moba_core(q, pos, kblk, vblk, kmean):
    tq = q.shape[0]
    nblk = kblk.shape[1]
    ksel = min(MOBA_TOPK, nblk)
    qg = q.reshape(tq, A_KV_HEADS, A_GROUP, HEAD_DIM)
    gate = jnp.einsum('qkgd,knd->qkgn', qg.astype(F32), kmean)
    bq = pos // MOBA_BLOCK
    past = jnp.arange(nblk)[None, :] < bq[:, None]
    gate = jnp.where(past[:, None, None, :], gate, NEG_INF)
    _, idx = lax.top_k(gate, ksel)
    sel_ok = idx < bq[:, None, None, None]
    kv_i = jnp.arange(A_KV_HEADS)
    k_sel = kblk[kv_i[None, :, None, None], idx]
    v_sel = vblk[kv_i[None, :, None, None], idx]
    k_own = kblk[kv_i[None, :], bq[:, None]]
    v_own = vblk[kv_i[None, :], bq[:, None]]
    own_pos = bq[:, None] * MOBA_BLOCK + jnp.arange(MOBA_BLOCK)[None, :]
    own_ok = own_pos <= pos[:, None]
    s_sel = jnp.einsum('qkgd,qkgsrd->qkgsr', qg, k_sel).astype(F32) * SCALE
    s_sel = jnp.where(sel_ok[..., None], s_sel, NEG_INF).reshape(tq, A_KV_HEADS, A_GROUP, ksel * MOBA_BLOCK)
    s_own = jnp.einsum('qkgd,qkrd->qkgr', qg, k_own).astype(F32) * SCALE
    s_own = jnp.where(own_ok[:, None, None, :], s_own, NEG_INF)
    p = jax.nn.softmax(jnp.concatenate([s_sel, s_own], axis=-1), axis=-1).astype(vblk.dtype)
    p_sel = p[..., :ksel * MOBA_BLOCK].reshape(tq, A_KV_HEADS, A_GROUP, ksel, MOBA_BLOCK)
    p_own = p[..., ksel * MOBA_BLOCK:]
    o = jnp.einsum('qkgsr,qkgsrd->qkgd', p_sel, v_sel) + jnp.einsum('qkgr,qkrd->qkgd', p_own, v_own)
    return o.reshape(tq, A_HEADS, HEAD_DIM)


def moba_prompt(q, k, v):
    b, s = q.shape[:2]
    nq = s // Q_CHUNK
    kblk = jax.vmap(to_blocks)(k)
    vblk = jax.vmap(to_blocks)(v)
    kmean = jax.vmap(block_means)(kblk)
    qc = q.reshape(b * nq, Q_CHUNK, A_HEADS, HEAD_DIM)
    bid = jnp.repeat(jnp.arange(b, dtype=jnp.int32), nq)
    start = jnp.tile(jnp.arange(nq, dtype=jnp.int32) * Q_CHUNK, b)

    def body(args):
        qb, bi, s0 = args
        pos = s0 + jnp.arange(Q_CHUNK, dtype=jnp.int32)
        return moba_core(qb, pos, kblk[bi], vblk[bi], kmean[bi])

    o = lax.map(body, (qc, bid, start))
    return o.reshape(b, s, A_HEADS, HEAD_DIM)


def moba_sample(q, k_new, v_new, cache_k, cache_v, layer, page_table):
    t = q.shape[1]
    past = page_table.shape[1] * PAGE_SIZE
    pos = past + jnp.arange(t, dtype=jnp.int32)

    def body(args):
        qn, kn, vn, pt = args
        k_seq = jnp.concatenate([cache_k[layer, pt].reshape(past, A_KV_HEADS, HEAD_DIM), kn], axis=0)
        v_seq = jnp.concatenate([cache_v[layer, pt].reshape(past, A_KV_HEADS, HEAD_DIM), vn], axis=0)
        kblk = to_blocks(k_seq)
        vblk = to_blocks(v_seq)
        return moba_core(qn, pos, kblk, vblk, block_means(kblk))

    return lax.map(body, (q, k_new, v_new, page_table))


def shared_kv(h, pos, g_kv, w_kv, kn_b):
    n, t, _ = h.shape
    z = rmsnorm(h, g_kv) @ w_kv
    k = z[..., :B_KV_WIDTH].reshape(n, t, B_KV_HEADS, HEAD_DIM)
    v = z[..., B_KV_WIDTH:].reshape(n, t, B_KV_HEADS, HEAD_DIM)
    return rope(rmsnorm(k, kn_b), pos), v


def proj_b(h, pos, g, w_in, qn):
    n, t, _ = h.shape
    z = rmsnorm(h, g) @ w_in
    q = z[..., :B_WIDTH].reshape(n, t, B_HEADS, HEAD_DIM)
    return rope(rmsnorm(q, qn), pos), z[..., B_WIDTH:]


def sink_probs(s, sinks):
    sk = sinks.astype(F32).reshape(B_KV_HEADS, B_GROUP, 1, 1)
    m = jnp.maximum(s.max(axis=-1, keepdims=True), sk)
    e = jnp.exp(s - m)
    return e / (e.sum(axis=-1, keepdims=True) + jnp.exp(sk - m))


def swa_prompt(q, k, v, sinks):
    b, s = q.shape[:2]
    nb = s // WINDOW
    qb = q.reshape(b, nb, WINDOW, B_KV_HEADS, B_GROUP, HEAD_DIM)
    kb = k.reshape(b, nb, WINDOW, B_KV_HEADS, HEAD_DIM)
    vb = v.reshape(b, nb, WINDOW, B_KV_HEADS, HEAD_DIM)
    shift = lambda a: jnp.pad(a, ((0, 0), (1, 0), (0, 0), (0, 0), (0, 0)))[:, :-1]
    kk = jnp.concatenate([shift(kb), kb], axis=2)
    vv = jnp.concatenate([shift(vb), vb], axis=2)
    i = jnp.arange(WINDOW)[:, None]
    j = jnp.arange(2 * WINDOW)[None, :]
    band = (j >= i) & (j <= i + WINDOW)
    has_prev = jnp.arange(nb)[:, None, None] > 0
    mask = band[None] & (has_prev | (j >= WINDOW)[None])
    sc = jnp.einsum('bnqkgd,bnjkd->bnkgqj', qb, kk).astype(F32) * SCALE
    sc = jnp.where(mask[None, :, None, None], sc, NEG_INF)
    p = sink_probs(sc, sinks).astype(v.dtype)
    o = jnp.einsum('bnkgqj,bnjkd->bnqkgd', p, vv)
    return o.reshape(b, s, B_HEADS, HEAD_DIM)


def swa_sample(q, k_new, v_new, buf_k, buf_v, sinks, past):
    n, t = q.shape[:2]
    kk = jnp.concatenate([buf_k, k_new], axis=1)
    vv = jnp.concatenate([buf_v, v_new], axis=1)
    kpos = past - WINDOW + jnp.arange(WINDOW + t)
    qpos = past + jnp.arange(t)
    d = qpos[:, None] - kpos[None, :]
    mask = (d >= 0) & (d <= WINDOW)
    qg = q.reshape(n, t, B_KV_HEADS, B_GROUP, HEAD_DIM)
    sc = jnp.einsum('ntkgd,njkd->nkgtj', qg, kk).astype(F32) * SCALE
    sc = jnp.where(mask, sc, NEG_INF)
    p = sink_probs(sc, sinks).astype(vv.dtype)
    o = jnp.einsum('nkgtj,njkd->ntkgd', p, vv).reshape(n, t, B_HEADS, HEAD_DIM)
    return o, kk[:, -WINDOW:], vv[:, -WINDOW:]


def setup_inputs(seed: int = 0) -> dict:
    key = jax.random.key(seed)
    ks = jax.random.split(key, 20)
    n_pages = PAST_LEN // PAGE_SIZE
    used = DEC_BATCH * n_pages
    n_pool = used + max(1, used // 4)
    page_table = jax.random.permutation(ks[0], n_pool)[:used].reshape(DEC_BATCH, n_pages).astype(jnp.int32)
    nrm = lambda k, shape, s=1.0: s * jax.random.normal(k, shape, F32)
    gain = lambda k, shape: 1.0 + 0.1 * jax.random.normal(k, shape, F32)
    a_in = 2 * A_WIDTH + 2 * A_KV_WIDTH
    return {
        'x_prompt': nrm(ks[1], (BATCH, SEQ, D_MODEL)),
        'x_sample': nrm(ks[2], (DEC_BATCH, DEC_SEQ, D_MODEL)),
        'cache_a_k': nrm(ks[3], (N_A_LAYERS, n_pool, PAGE_SIZE, A_KV_HEADS, HEAD_DIM)),
        'cache_a_v': nrm(ks[4], (N_A_LAYERS, n_pool, PAGE_SIZE, A_KV_HEADS, HEAD_DIM)),
        'state_b_k': nrm(ks[5], (DEC_BATCH, WINDOW, B_KV_HEADS, HEAD_DIM)),
        'state_b_v': nrm(ks[6], (DEC_BATCH, WINDOW, B_KV_HEADS, HEAD_DIM)),
        'page_table': page_table,
        'g_a': gain(ks[7], (N_A_LAYERS, D_MODEL)),
        'w_in_a': nrm(ks[8], (N_A_LAYERS, D_MODEL, a_in), D_MODEL ** -0.5),
        'qn_a': gain(ks[9], (N_A_LAYERS, HEAD_DIM)),
        'kn_a': gain(ks[10], (N_A_LAYERS, HEAD_DIM)),
        'w_out_a': nrm(ks[11], (N_A_LAYERS, A_WIDTH, D_MODEL), A_WIDTH ** -0.5),
        'g_kv': gain(ks[12], (D_MODEL,)),
        'w_kv': nrm(ks[13], (D_MODEL, 2 * B_KV_WIDTH), D_MODEL ** -0.5),
        'kn_b': gain(ks[14], (HEAD_DIM,)),
        'g_b': gain(ks[15], (N_B_LAYERS, D_MODEL)),
        'w_in_b': nrm(ks[16], (N_B_LAYERS, D_MODEL, 2 * B_WIDTH), D_MODEL ** -0.5),
        'qn_b': gain(ks[17], (N_B_LAYERS, HEAD_DIM)),
        'sinks_b': nrm(ks[18], (N_B_LAYERS, B_HEADS), 0.5),
        'w_out_b': nrm(ks[19], (N_B_LAYERS, B_WIDTH, D_MODEL), B_WIDTH ** -0.5),
    }


def reference(x_prompt, x_sample, cache_a_k, cache_a_v, state_b_k, state_b_v, page_table,
              g_a, w_in_a, qn_a, kn_a, w_out_a, g_kv, w_kv, kn_b,
              g_b, w_in_b, qn_b, sinks_b, w_out_b):
    s = x_prompt.shape[1]
    t = x_sample.shape[1]
    past = page_table.shape[1] * PAGE_SIZE
    pos_p = jnp.arange(s, dtype=jnp.int32)
    pos_s = past + jnp.arange(t, dtype=jnp.int32)
    hp, hs = x_prompt, x_sample
    pa_k, pa_v, sa_k, sa_v = [], [], [], []
    for layer in range(DEPTH):
        if layer < N_A_LAYERS:
            a = layer
            qp, kp, vp, gp = proj_a(hp, pos_p, g_a[a], w_in_a[a], qn_a[a], kn_a[a])
            qs, ks_, vs, gs = proj_a(hs, pos_s, g_a[a], w_in_a[a], qn_a[a], kn_a[a])
            hp = hp + gated_out(moba_prompt(qp, kp, vp), gp, w_out_a[a])
            hs = hs + gated_out(moba_sample(qs, ks_, vs, cache_a_k, cache_a_v, a, page_table), gs, w_out_a[a])
            pa_k.append(kp)
            pa_v.append(vp)
            sa_k.append(ks_)
            sa_v.append(vs)
        else:
            if layer == N_A_LAYERS:
                kbp, vbp = shared_kv(hp, pos_p, g_kv, w_kv, kn_b)
                kbs, vbs = shared_kv(hs, pos_s, g_kv, w_kv, kn_b)
                win_pk, win_pv = kbp[:, -WINDOW:], vbp[:, -WINDOW:]
            bl = layer - N_A_LAYERS
            qp, gp = proj_b(hp, pos_p, g_b[bl], w_in_b[bl], qn_b[bl])
            qs, gs = proj_b(hs, pos_s, g_b[bl], w_in_b[bl], qn_b[bl])
            o_s, win_sk, win_sv = swa_sample(qs, kbs, vbs, state_b_k, state_b_v, sinks_b[bl], past)
            hp = hp + gated_out(swa_prompt(qp, kbp, vbp, sinks_b[bl]), gp, w_out_b[bl])
            hs = hs + gated_out(o_s, gs, w_out_b[bl])
    return (hp, hs, jnp.stack(pa_k), jnp.stack(pa_v), win_pk, win_pv,
            jnp.stack(sa_k), jnp.stack(sa_v), win_sk, win_sv)
```

```python
import functools

import jax
import jax.numpy as jnp
from jax import lax
from jax.experimental import pallas as pl
from jax.experimental.pallas import tpu as pltpu

F32 = jnp.float32
BF16 = jnp.bfloat16

HEAD_DIM = 64
HALF = HEAD_DIM // 2
A_KV_HEADS = 4
B_KV_HEADS = 2
MOBA_BLOCK = 256
MOBA_TOPK = 3
PAGE_SIZE = 128
WINDOW = 128
ROPE_THETA = 10000.0
EPS = 1e-6
NEG_INF = -1e30
SCALE = HEAD_DIM ** -0.5

LANES = 128
HEADS_PER_VREG = LANES // HEAD_DIM
VMEM_LIMIT = 56 * 1024 * 1024

PAGES_PER_CHUNK = 32
RING = 4
LOOKAHEAD = RING - 1


def _nt(a, b):
    return lax.dot_general(a, b, (((1,), (1,)), ((), ())), preferred_element_type=F32)


def _split(x):
    hi = x.astype(BF16)
    lo = (x - hi.astype(F32)).astype(BF16)
    return hi, lo


def _nt3(a, b_split):
    a_hi, a_lo = _split(a)
    b_hi, b_lo = b_split
    return _nt(a_hi, b_hi) + _nt(a_hi, b_lo) + _nt(a_lo, b_hi)


def _silu(g):
    return g * (1.0 / (1.0 + jnp.exp(-g)))


def _rms_scale(x):
    return lax.rsqrt(jnp.mean(x * x, axis=-1, keepdims=True) + EPS)


def _head_norm_rope(zz, gain, cos, sin, b2, first_half):
    ss = jnp.dot((zz * zz).astype(BF16), b2, preferred_element_type=F32)
    zh = zz * lax.rsqrt(ss * (1.0 / HEAD_DIM) + EPS) * gain
    partner = jnp.where(first_half, pltpu.roll(zh, LANES - HALF, 1), pltpu.roll(zh, HALF, 1))
    return zh * cos + partner * sin


def _top3_keep(g, valid, lane):
    cand = jnp.where(valid, g, NEG_INF)
    sel = jnp.zeros(g.shape, jnp.bool_)
    for _ in range(MOBA_TOPK):
        mx = jnp.max(cand, axis=1, keepdims=True)
        first = jnp.min(jnp.where(cand == mx, lane, 2 * LANES), axis=1, keepdims=True)
        pick = lane == first
        sel = sel | pick
        cand = jnp.where(pick, -jnp.inf, cand)
    return sel & valid


def _proj_a_prompt_kernel(x_ref, w_ref, g_ref, gain_ref, cos_ref, sin_ref, b2_ref,
                          qa_ref, ka_ref, va_ref, k_ref, v_ref, gate_ref, kmt_ref,
                          *, n_heads, n_kv):
    i = pl.program_id(0)
    tm = x_ref.shape[0]
    a_width = n_heads * HEAD_DIM
    kv_width = n_kv * HEAD_DIM

    @pl.when(i == 0)
    def _():
        kmt_ref[...] = jnp.zeros(kmt_ref.shape, F32)

    x = x_ref[...]
    xb = (x * _rms_scale(x) * g_ref[...]).astype(BF16)
    z = jnp.dot(xb, w_ref[...], preferred_element_type=F32)

    lane = lax.broadcasted_iota(jnp.int32, (tm, LANES), 1)
    first_half = (lane % HEAD_DIM) < HALF
    low = lane < HEAD_DIM
    cos = cos_ref[...]
    sin = sin_ref[...]
    b2 = b2_ref[...]
    blk_lane = lane - HEAD_DIM
    past = (lane >= HEAD_DIM) & (blk_lane < i)
    own = (blk_lane == i) | (lane == LANES - 1)

    group = n_heads // n_kv
    kmt_split = [_split(kmt_ref[kvh]) for kvh in range(n_kv)]
    for c in range(a_width // LANES):
        ro = _head_norm_rope(z[:, c * LANES:(c + 1) * LANES], gain_ref[:, c * LANES:(c + 1) * LANES],
                             cos, sin, b2, first_half)
        g2 = _nt3(ro, kmt_split[(c * HEADS_PER_VREG) // group])
        qs = ro * SCALE
        qs_r = pltpu.roll(qs, HEAD_DIM, 1)
        for e in range(HEADS_PER_VREG):
            keep = _top3_keep(g2[:, e * LANES:(e + 1) * LANES], past, lane) | own
            bias = jnp.where(keep, 0.0, NEG_INF)
            qa = jnp.where(low, qs if e == 0 else qs_r, bias)
            qa_ref[c * HEADS_PER_VREG + e] = qa.astype(BF16)

    onehot = jnp.where(own, 1.0, 0.0)
    for c in range(kv_width // LANES):
        col = a_width + c * LANES
        ro = _head_norm_rope(z[:, col:col + LANES], gain_ref[:, col:col + LANES], cos, sin, b2, first_half)
        k_ref[:, c * LANES:(c + 1) * LANES] = ro
        ro_r = pltpu.roll(ro, HEAD_DIM, 1)
        cs = jnp.sum(ro, axis=0, keepdims=True) * (1.0 / MOBA_BLOCK)
        cs_r = jnp.sum(ro_r, axis=0, keepdims=True) * (1.0 / MOBA_BLOCK)
        low1 = low[0:1]
        for e in range(HEADS_PER_VREG):
            kvh = c * HEADS_PER_VREG + e
            ka_ref[kvh] = jnp.where(low, ro if e == 0 else ro_r, onehot).astype(BF16)
            kmt_ref[kvh, pl.ds(HEAD_DIM + i, 1), :] = jnp.where(low1, cs if e == 0 else cs_r, 0.0)
            kmt_ref[kvh, pl.ds(LANES + HEAD_DIM + i, 1), :] = jnp.where(low1, 0.0, cs_r if e == 0 else cs)

    ones_col = jnp.where(lane == HEAD_DIM, 1.0, 0.0)
    v0 = a_width + kv_width
    v_ref[...] = z[:, v0:v0 + kv_width]
    for c in range(kv_width // LANES):
        vv = z[:, v0 + c * LANES:v0 + (c + 1) * LANES]
        vv_r = pltpu.roll(vv, HEAD_DIM, 1)
        for e in range(HEADS_PER_VREG):
            va_ref[c * HEADS_PER_VREG + e] = jnp.where(low, vv if e == 0 else vv_r, ones_col).astype(BF16)

    gate_ref[...] = z[:, v0 + kv_width:]


def _proj_a_sample_kernel(x_ref, w_ref, g_ref, gain_ref, cos_ref, sin_ref, b2_ref,
                          q_ref, k_ref, v_ref, gate_ref, *, n_heads, n_kv):
    tm = x_ref.shape[0]
    a_width = n_heads * HEAD_DIM
    kv_width = n_kv * HEAD_DIM
    x = x_ref[...]
    xb = (x * _rms_scale(x) * g_ref[...]).astype(BF16)
    z = jnp.dot(xb, w_ref[...], preferred_element_type=F32)
    lane = lax.broadcasted_iota(jnp.int32, (tm, LANES), 1)
    first_half = (lane % HEAD_DIM) < HALF
    cos = cos_ref[...]
    sin = sin_ref[...]
    b2 = b2_ref[...]
    for c in range((a_width + kv_width) // LANES):
        ro = _head_norm_rope(z[:, c * LANES:(c + 1) * LANES], gain_ref[:, c * LANES:(c + 1) * LANES],
                             cos, sin, b2, first_half)
        if c * LANES < a_width:
            q_ref[:, c * LANES:(c + 1) * LANES] = ro
        else:
            k_ref[:, c * LANES - a_width:(c + 1) * LANES - a_width] = ro
    v0 = a_width + kv_width
    v_ref[...] = z[:, v0:v0 + kv_width]
    gate_ref[...] = z[:, v0 + kv_width:]


def _const_spec(shape):
    return pl.BlockSpec(shape, lambda *_: (0,) * len(shape))


def _proj_a_prompt(x, w, g, gain, cos, sin, b2, n_heads, n_kv):
    s, d = x.shape
    tm = MOBA_BLOCK
    n_in = w.shape[1]
    a_width = n_heads * HEAD_DIM
    kv_width = n_kv * HEAD_DIM
    row = lambda width: pl.BlockSpec((tm, width), lambda i: (i, 0))
    head_plane = lambda n: pl.BlockSpec((n, tm, LANES), lambda i: (0, i, 0))
    return pl.pallas_call(
        functools.partial(_proj_a_prompt_kernel, n_heads=n_heads, n_kv=n_kv),
        grid=(s // tm,),
        in_specs=[row(d), _const_spec((d, n_in)), _const_spec((1, d)), _const_spec((1, a_width + kv_width)),
                  row(LANES), row(LANES), _const_spec((LANES, LANES))],
        out_specs=[head_plane(n_heads), head_plane(n_kv), head_plane(n_kv),
                   row(kv_width), row(kv_width), row(a_width)],
        out_shape=[jax.ShapeDtypeStruct((n_heads, s, LANES), BF16),
                   jax.ShapeDtypeStruct((n_kv, s, LANES), BF16),
                   jax.ShapeDtypeStruct((n_kv, s, LANES), BF16),
                   jax.ShapeDtypeStruct((s, kv_width), F32),
                   jax.ShapeDtypeStruct((s, kv_width), F32),
                   jax.ShapeDtypeStruct((s, a_width), F32)],
        scratch_shapes=[pltpu.VMEM((n_kv, 2 * LANES, LANES), F32)],
        compiler_params=pltpu.CompilerParams(dimension_semantics=("arbitrary",),
                                             vmem_limit_bytes=VMEM_LIMIT),
        name="proj_a_prompt",
    )(x, w, g, gain, cos, sin, b2)


def _proj_a_sample(x, w, g, gain, cos, sin, b2, n_heads, n_kv):
    n, d = x.shape
    a_width = n_heads * HEAD_DIM
    kv_width = n_kv * HEAD_DIM
    full = lambda a: _const_spec(a.shape)
    out_shape = [jax.ShapeDtypeStruct((n, a_width), F32), jax.ShapeDtypeStruct((n, kv_width), F32),
                 jax.ShapeDtypeStruct((n, kv_width), F32), jax.ShapeDtypeStruct((n, a_width), F32)]
    return pl.pallas_call(
        functools.partial(_proj_a_sample_kernel, n_heads=n_heads, n_kv=n_kv),
        grid=(1,),
        in_specs=[full(a) for a in (x, w, g, gain, cos, sin, b2)],
        out_specs=[_const_spec(o.shape) for o in out_shape],
        out_shape=out_shape,
        compiler_params=pltpu.CompilerParams(vmem_limit_bytes=VMEM_LIMIT),
        name="proj_a_sample",
    )(x, w, g, gain, cos, sin, b2)


def _moba_prompt_kernel(q_ref, k_ref, v_ref, o_ref, m_ref, acc_ref, q2_ref):
    i = pl.program_id(1)
    group, tq, _ = q_ref.shape
    blk = MOBA_BLOCK
    row = lax.broadcasted_iota(jnp.int32, (tq, blk), 0)
    col = lax.broadcasted_iota(jnp.int32, (tq, blk), 1)
    causal = col <= row
    lane = lax.broadcasted_iota(jnp.int32, (tq, LANES), 1)
    d0 = pl.multiple_of(i * blk, blk)
    kd = k_ref[pl.ds(d0, blk), :]
    vd = v_ref[pl.ds(d0, blk), :]

    for g in range(group):
        sd = jnp.where(causal, _nt(q_ref[g], kd), NEG_INF)
        m_ref[g] = jnp.maximum(sd[:, :LANES], sd[:, LANES:])

    def max_body(j, carry):
        kj = k_ref[pl.ds(pl.multiple_of(j * blk, blk), blk), :]
        for g in range(group):
            s = _nt(q_ref[g], kj)
            m_ref[g] = jnp.maximum(m_ref[g], jnp.maximum(s[:, :LANES], s[:, LANES:]))
        return carry

    lax.fori_loop(0, i, max_body, 0)

    for g in range(group):
        m = jnp.max(m_ref[g], axis=1, keepdims=True).astype(BF16)
        q2_ref[g] = jnp.where(lane == LANES - 1, -m, q_ref[g])
        pd = jnp.exp(jnp.where(causal, _nt(q2_ref[g], kd), NEG_INF))
        acc_ref[g] = jnp.dot(pd.astype(BF16), vd, preferred_element_type=F32)

    def pv_body(j, carry):
        j0 = pl.multiple_of(j * blk, blk)
        kj = k_ref[pl.ds(j0, blk), :]
        vj = v_ref[pl.ds(j0, blk), :]
        for g in range(group):
            p = jnp.exp(_nt(q2_ref[g], kj))
            acc_ref[g] += jnp.dot(p.astype(BF16), vj, preferred_element_type=F32)
        return carry

    lax.fori_loop(0, i, pv_body, 0)

    for g in range(group):
        acc = acc_ref[g]
        o_ref[:, g * HEAD_DIM:(g + 1) * HEAD_DIM] = acc[:, :HEAD_DIM] / acc[:, HEAD_DIM:HEAD_DIM + 1]


def _moba_prompt(qa, ka, va):
    n_heads, s, _ = qa.shape
    n_kv = ka.shape[0]
    group = n_heads // n_kv
    tq = MOBA_BLOCK
    return pl.pallas_call(
        _moba_prompt_kernel,
        grid=(n_kv, s // tq),
        in_specs=[pl.BlockSpec((group, tq, LANES), lambda c, i: (c, i, 0)),
                  pl.BlockSpec((None, s, LANES), lambda c, i: (c, 0, 0)),
                  pl.BlockSpec((None, s, LANES), lambda c, i: (c, 0, 0))],
        out_specs=pl.BlockSpec((tq, group * HEAD_DIM), lambda c, i: (i, c)),
        out_shape=jax.ShapeDtypeStruct((s, n_heads * HEAD_DIM), F32),
        scratch_shapes=[pltpu.VMEM((group, tq, LANES), F32), pltpu.VMEM((group, tq, LANES), F32),
                        pltpu.VMEM((group, tq, LANES), BF16)],
        compiler_params=pltpu.CompilerParams(dimension_semantics=("arbitrary", "arbitrary"),
                                             vmem_limit_bytes=VMEM_LIMIT),
        name="moba_prompt",
    )(qa, ka, va)


def _post_a_kernel(o_ref, gate_ref, x_ref, wo_ref, gkv_ref, wkv_ref, knb_ref, gb_ref, wb_ref, qnb_ref,
                   cos_ref, sin_ref, b2_ref, h_ref, kb_ref, vb_ref, qb_ref, gateb_ref):
    tm = x_ref.shape[0]
    b_width = qb_ref.shape[1]
    h = x_ref[...] + jnp.dot((o_ref[...] * _silu(gate_ref[...])).astype(BF16), wo_ref[...],
                             preferred_element_type=F32)
    h_ref[...] = h
    hn = h * _rms_scale(h)
    lane = lax.broadcasted_iota(jnp.int32, (tm, LANES), 1)
    first_half = (lane % HEAD_DIM) < HALF
    cos = cos_ref[...]
    sin = sin_ref[...]
    b2 = b2_ref[...]

    zkv = jnp.dot((hn * gkv_ref[...]).astype(BF16), wkv_ref[...], preferred_element_type=F32)
    kvw = kb_ref.shape[1]
    for c in range(kvw // LANES):
        kb_ref[:, c * LANES:(c + 1) * LANES] = _head_norm_rope(
            zkv[:, c * LANES:(c + 1) * LANES], knb_ref[...], cos, sin, b2, first_half)
    vb_ref[...] = zkv[:, kvw:]

    zb = jnp.dot((hn * gb_ref[...]).astype(BF16), wb_ref[...], preferred_element_type=F32)
    for c in range(b_width // LANES):
        ro = _head_norm_rope(zb[:, c * LANES:(c + 1) * LANES], qnb_ref[...], cos, sin, b2, first_half)
        qb_ref[:, c * LANES:(c + 1) * LANES] = (ro * SCALE).astype(BF16)
    gateb_ref[...] = zb[:, b_width:]


def _post_a(o, gate, x, wo, gkv, wkv, knb, gb, wb, qnb, cos, sin, b2, tm):
    s, d = x.shape
    kvw = wkv.shape[1] // 2
    b_width = wb.shape[1] // 2
    row = lambda width: pl.BlockSpec((tm, width), lambda i: (i, 0))
    consts = (wo, gkv, wkv, knb, gb, wb, qnb)
    return pl.pallas_call(
        _post_a_kernel,
        grid=(s // tm,),
        in_specs=[row(o.shape[1]), row(gate.shape[1]), row(d)] + [_const_spec(a.shape) for a in consts]
                 + [row(LANES), row(LANES), _const_spec(b2.shape)],
        out_specs=[row(d), row(kvw), row(kvw), row(b_width), row(b_width)],
        out_shape=[jax.ShapeDtypeStruct((s, d), F32), jax.ShapeDtypeStruct((s, kvw), F32),
                   jax.ShapeDtypeStruct((s, kvw), F32), jax.ShapeDtypeStruct((s, b_width), BF16),
                   jax.ShapeDtypeStruct((s, b_width), F32)],
        compiler_params=pltpu.CompilerParams(dimension_semantics=("parallel",),
                                             vmem_limit_bytes=VMEM_LIMIT),
        name="post_a",
    )(o, gate, x, *consts, cos, sin, b2)


def _swa_prompt_kernel(q_ref, kc_ref, kp_ref, vc_ref, vp_ref, sink_ref, gate_ref, h_ref, wo_ref,
                       y_ref, o_scr):
    n = pl.program_id(0)
    w = q_ref.shape[0]
    n_heads = q_ref.shape[1] // HEAD_DIM
    n_kv = kc_ref.shape[1] // HEAD_DIM
    group = n_heads // n_kv
    kk = jnp.concatenate([kp_ref[...], kc_ref[...]], axis=0).astype(BF16)
    vv = jnp.concatenate([vp_ref[...], vc_ref[...]], axis=0).astype(BF16)
    rows = group * w
    t = lax.broadcasted_iota(jnp.int32, (rows, 2 * w), 0) % w
    j = lax.broadcasted_iota(jnp.int32, (rows, 2 * w), 1)
    mask = (j >= t) & (j <= t + w) & ((n > 0) | (j >= w))
    lane = lax.broadcasted_iota(jnp.int32, (w, LANES), 1)

    for c in range(n_kv):
        in_slot = (lane // HEAD_DIM) == c
        parts = []
        sinks = []
        for g in range(group):
            hd = c * group + g
            vreg = q_ref[:, (hd // HEADS_PER_VREG) * LANES:(hd // HEADS_PER_VREG + 1) * LANES].astype(F32)
            if hd % HEADS_PER_VREG != c:
                vreg = pltpu.roll(vreg, HEAD_DIM, 1)
            parts.append(jnp.where(in_slot, vreg, 0.0).astype(BF16))
            sinks.append(jnp.full((w, 1), sink_ref[hd], F32))
        qs = jnp.concatenate(parts, axis=0)
        sk = jnp.concatenate(sinks, axis=0)
        s = jnp.where(mask, _nt(qs, kk), NEG_INF)
        m = jnp.maximum(jnp.max(s, axis=1, keepdims=True), sk)
        e = jnp.exp(s - m)
        p = e / (jnp.sum(e, axis=1, keepdims=True) + jnp.exp(sk - m))
        o = jnp.dot(p.astype(BF16), vv, preferred_element_type=F32)
        for g in range(0, group, HEADS_PER_VREG):
            hd = c * group + g
            even = o[g * w:(g + 1) * w]
            odd = o[(g + 1) * w:(g + 2) * w]
            if c == 0:
                odd = pltpu.roll(odd, HEAD_DIM, 1)
            else:
                even = pltpu.roll(even, HEAD_DIM, 1)
            dst = (hd // HEADS_PER_VREG) * LANES
            o_scr[:, dst:dst + LANES] = jnp.where(lane < HEAD_DIM, even, odd)

    y_ref[...] = h_ref[...] + jnp.dot((o_scr[...] * _silu(gate_ref[...])).astype(BF16), wo_ref[...],
                                      preferred_element_type=F32)


def _swa_prompt(qb, kb, vb, sinks, gateb, h, wo):
    s, b_width = qb.shape
    kvw = kb.shape[1]
    d = h.shape[1]
    w = WINDOW
    cur = lambda width: pl.BlockSpec((w, width), lambda n: (n, 0))
    prev = lambda width: pl.BlockSpec((w, width), lambda n: (jnp.maximum(n - 1, 0), 0))
    return pl.pallas_call(
        _swa_prompt_kernel,
        grid=(s // w,),
        in_specs=[cur(b_width), cur(kvw), prev(kvw), cur(kvw), prev(kvw),
                  pl.BlockSpec(memory_space=pltpu.SMEM), cur(b_width), cur(d), _const_spec(wo.shape)],
        out_specs=cur(d),
        out_shape=jax.ShapeDtypeStruct((s, d), F32),
        scratch_shapes=[pltpu.VMEM((w, b_width), F32)],
        compiler_params=pltpu.CompilerParams(dimension_semantics=("parallel",),
                                             vmem_limit_bytes=VMEM_LIMIT),
        name="swa_prompt",
    )(qb, kb, kb, vb, vb, sinks, gateb, h, wo)


def _moba_sample_kernel(pt_ref, wq_ref, knew_ref, vnew_ref, kc_ref, vc_ref,
                        o_ref, buf, sem, s_scr, p_scr, ksum_scr):
    b = pl.program_id(0)
    n_samples = pl.num_programs(0)
    n_heads = wq_ref.shape[1]
    kvw = wq_ref.shape[2]
    n_blocks = s_scr.shape[0] * PAGE_SIZE // MOBA_BLOCK
    blocks_per_chunk = PAGES_PER_CHUNK * PAGE_SIZE // MOBA_BLOCK
    assert MOBA_BLOCK == 2 * PAGE_SIZE and n_blocks <= LANES
    jobs = 2 * RING

    def page_copy(sample, job, p):
        cache = kc_ref if job < RING else vc_ref
        slot = job % RING
        page = pt_ref[sample, (job % RING) * PAGES_PER_CHUNK + p]
        return pltpu.make_async_copy(cache.at[page], buf.at[slot, p], sem.at[slot])

    def issue(sample, job):
        def body(p, carry):
            page_copy(sample, job, p).start()
            return carry
        lax.fori_loop(0, PAGES_PER_CHUNK, body, 0)

    def wait(sample, job):
        def body(p, carry):
            page_copy(sample, job, p).wait()
            return carry
        lax.fori_loop(0, PAGES_PER_CHUNK, body, 0)

    @pl.when(b == 0)
    def _():
        ksum_scr[...] = jnp.zeros(ksum_scr.shape, F32)
        for job in range(LOOKAHEAD):
            issue(b, job)

    def prefetch(job):
        nxt = job + LOOKAHEAD
        if nxt < jobs:
            issue(b, nxt)
        else:
            @pl.when(b + 1 < n_samples)
            def _():
                issue(b + 1, nxt - jobs)

    wq = wq_ref[0]
    wqs = (wq * SCALE).astype(BF16)

    lane_k = lax.broadcasted_iota(jnp.int32, (kvw, LANES), 1)
    for job in range(RING):
        wait(b, job)
        prefetch(job)

        def k_body(r, carry, job=job):
            kt0 = buf[job, 2 * r]
            kt1 = buf[job, 2 * r + 1]
            pg = job * PAGES_PER_CHUNK + 2 * r
            s_scr[pg] = jnp.dot(wqs, kt0.astype(BF16), preferred_element_type=F32)
            s_scr[pg + 1] = jnp.dot(wqs, kt1.astype(BF16), preferred_element_type=F32)
            col = jnp.sum(kt0 + kt1, axis=1, keepdims=True)
            ksum_scr[...] = jnp.where(lane_k == job * blocks_per_chunk + r, col, ksum_scr[...])
            return carry

        lax.fori_loop(0, blocks_per_chunk, k_body, 0)

    kmean = ksum_scr[...] * (1.0 / MOBA_BLOCK)
    wq_hi, wq_lo = _split(wq)
    km_hi, km_lo = _split(kmean)
    dot = functools.partial(jnp.dot, preferred_element_type=F32)
    gate = dot(wq_hi, km_hi) + dot(wq_hi, km_lo) + dot(wq_lo, km_hi)
    lane = lax.broadcasted_iota(jnp.int32, gate.shape, 1)
    keep = _top3_keep(gate, lane < n_blocks, lane)
    bias = jnp.where(keep, 0.0, NEG_INF)

    def block_bias(r):
        return jnp.sum(jnp.where(lane == r, bias, 0.0), axis=1, keepdims=True)

    def max_body(r, ml):
        return jnp.maximum(ml, jnp.maximum(s_scr[2 * r], s_scr[2 * r + 1]) + block_bias(r))

    ml = lax.fori_loop(0, n_blocks, max_body, jnp.full((n_heads, LANES), NEG_INF, F32))
    s_self = jnp.sum(wq * knew_ref[0], axis=1, keepdims=True) * SCALE
    m = jnp.maximum(jnp.max(ml, axis=1, keepdims=True), s_self)

    def exp_body(r, ll):
        shift = block_bias(r) - m
        for u in range(2):
            p = jnp.exp(s_scr[2 * r + u] + shift)
            p_scr[2 * r + u] = p.astype(BF16)
            ll = ll + p
        return ll

    ll = lax.fori_loop(0, n_blocks, exp_body, jnp.zeros((n_heads, LANES), F32))
    p_self = jnp.exp(s_self - m)
    denom = jnp.sum(ll, axis=1, keepdims=True) + p_self

    acc = jnp.zeros((n_heads, kvw), F32)
    for job in range(RING, jobs):
        wait(b, job)
        prefetch(job)

        def v_body(pp, acc, job=job):
            vt = buf[job % RING, pp]
            return acc + _nt(p_scr[(job - RING) * PAGES_PER_CHUNK + pp], vt.astype(BF16))

        acc = lax.fori_loop(0, PAGES_PER_CHUNK, v_body, acc)

    o = (acc + p_self * vnew_ref[0]) / denom
    head = lax.broadcasted_iota(jnp.int32, (n_heads, HEAD_DIM), 0)
    group = n_heads // (kvw // HEAD_DIM)
    out = jnp.zeros((n_heads, HEAD_DIM), F32)
    for c in range(kvw // HEAD_DIM):
        out = jnp.where(head // group == c, o[:, c * HEAD_DIM:(c + 1) * HEAD_DIM], out)
    o_ref[0] = out


def _moba_sample(page_table, wq, knew, vnew, cache_kt, cache_vt):
    n, n_heads, kvw = wq.shape
    n_pages = page_table.shape[1]
    assert n_pages == RING * PAGES_PER_CHUNK and cache_kt.shape[1:] == (kvw, PAGE_SIZE)
    grid_spec = pltpu.PrefetchScalarGridSpec(
        num_scalar_prefetch=1,
        grid=(n,),
        in_specs=[pl.BlockSpec((1, n_heads, kvw), lambda b, pt: (b, 0, 0)),
                  pl.BlockSpec((1, 1, kvw), lambda b, pt: (b, 0, 0)),
                  pl.BlockSpec((1, 1, kvw), lambda b, pt: (b, 0, 0)),
                  pl.BlockSpec(memory_space=pl.ANY),
                  pl.BlockSpec(memory_space=pl.ANY)],
        out_specs=pl.BlockSpec((1, n_heads, HEAD_DIM), lambda b, pt: (b, 0, 0)),
        scratch_shapes=[pltpu.VMEM((RING, PAGES_PER_CHUNK, kvw, PAGE_SIZE), F32),
                        pltpu.SemaphoreType.DMA((RING,)),
                        pltpu.VMEM((n_pages, n_heads, PAGE_SIZE), F32),
                        pltpu.VMEM((n_pages, n_heads, PAGE_SIZE), BF16),
                        pltpu.VMEM((kvw, LANES), F32)],
    )
    return pl.pallas_call(
        _moba_sample_kernel,
        grid_spec=grid_spec,
        out_shape=jax.ShapeDtypeStruct((n, n_heads, HEAD_DIM), F32),
        compiler_params=pltpu.CompilerParams(dimension_semantics=("arbitrary",),
                                             vmem_limit_bytes=VMEM_LIMIT),
        name="moba_sample",
    )(page_table, wq, knew, vnew, cache_kt, cache_vt)


def _swa_sample_kernel(wq_ref, sk_ref, sv_ref, knew_ref, vnew_ref, knewt_ref, vnewt_ref, sink_ref,
                       o_ref, wk_ref, wv_ref):
    nb, n_heads, kvw = wq_ref.shape
    w = sk_ref.shape[2]
    group = n_heads // (kvw // HEAD_DIM)
    head = lax.broadcasted_iota(jnp.int32, (n_heads, HEAD_DIM), 0)
    pos = lax.broadcasted_iota(jnp.int32, (kvw, w), 1)
    col_id = lax.broadcasted_iota(jnp.int32, knewt_ref.shape, 1)
    sink = sink_ref[...]
    for i in range(nb):
        wq = wq_ref[i]
        kt = sk_ref[i]
        vt = sv_ref[i]
        knew = knew_ref[i]
        vnew = vnew_ref[i]
        s = jnp.dot(wq, kt.astype(BF16), preferred_element_type=F32)
        s_new = jnp.sum(wq.astype(F32) * knew, axis=1, keepdims=True)
        m = jnp.maximum(jnp.maximum(jnp.max(s, axis=1, keepdims=True), s_new), sink)
        e = jnp.exp(s - m)
        e_new = jnp.exp(s_new - m)
        denom = jnp.sum(e, axis=1, keepdims=True) + e_new + jnp.exp(sink - m)
        o = _nt((e / denom).astype(BF16), vt.astype(BF16)) + (e_new / denom) * vnew
        out = jnp.zeros((n_heads, HEAD_DIM), F32)
        for c in range(kvw // HEAD_DIM):
            out = jnp.where(head // group == c, o[:, c * HEAD_DIM:(c + 1) * HEAD_DIM], out)
        o_ref[i] = out
        mine = col_id == pl.program_id(0) * nb + i
        knew_col = jnp.sum(jnp.where(mine, knewt_ref[...], 0.0), axis=1, keepdims=True)
        vnew_col = jnp.sum(jnp.where(mine, vnewt_ref[...], 0.0), axis=1, keepdims=True)
        wk_ref[i] = jnp.where(pos == w - 1, knew_col, pltpu.roll(kt, w - 1, 1))
        wv_ref[i] = jnp.where(pos == w - 1, vnew_col, pltpu.roll(vt, w - 1, 1))


def _swa_sample(wq, state_kt, state_vt, knew, vnew, sinks):
    n, n_heads, kvw = wq.shape
    w = state_kt.shape[2]
    nb = 8
    blk = lambda shape: pl.BlockSpec((nb,) + shape, lambda i: (i,) + (0,) * len(shape))
    knew_t = knew.T
    vnew_t = vnew.T
    return pl.pallas_call(
        _swa_sample_kernel,
        grid=(n // nb,),
        in_specs=[blk((n_heads, kvw)), blk((kvw, w)), blk((kvw, w)), blk((1, kvw)), blk((1, kvw)),
                  _const_spec(knew_t.shape), _const_spec(vnew_t.shape), _const_spec(sinks.shape)],
        out_specs=[blk((n_heads, HEAD_DIM)), blk((kvw, w)), blk((kvw, w))],
        out_shape=[jax.ShapeDtypeStruct((n, n_heads, HEAD_DIM), F32),
                   jax.ShapeDtypeStruct((n, kvw, w), F32), jax.ShapeDtypeStruct((n, kvw, w), F32)],
        compiler_params=pltpu.CompilerParams(dimension_semantics=("parallel",),
                                             vmem_limit_bytes=VMEM_LIMIT),
        name="swa_sample",
    )(wq, state_kt, state_vt, knew[:, None, :], vnew[:, None, :], knew_t, vnew_t, sinks)


def _gated_out_kernel(o_ref, gate_ref, h_ref, wo_ref, y_ref):
    y_ref[...] = h_ref[...] + jnp.dot((o_ref[...] * _silu(gate_ref[...])).astype(BF16), wo_ref[...],
                                      preferred_element_type=F32)


def _gated_out(o, gate, h, wo):
    return pl.pallas_call(
        _gated_out_kernel,
        grid=(1,),
        in_specs=[_const_spec(a.shape) for a in (o, gate, h, wo)],
        out_specs=_const_spec(h.shape),
        out_shape=jax.ShapeDtypeStruct(h.shape, F32),
        compiler_params=pltpu.CompilerParams(vmem_limit_bytes=VMEM_LIMIT),
        name="gated_out",
    )(o, gate, h, wo)


def _rope_tables(pos):
    inv = 1.0 / (ROPE_THETA ** (jnp.arange(HALF, dtype=F32) / HALF))
    ang = pos.astype(F32)[:, None] * inv[None, :]
    c, s = jnp.cos(ang), jnp.sin(ang)
    return jnp.tile(c, (1, LANES // HALF)), jnp.tile(jnp.concatenate([-s, s], axis=1), (1, HEADS_PER_VREG))


def _slot_rows(q, n_heads, n_kv):
    n = q.shape[0]
    group = n_heads // n_kv
    onehot = (jnp.arange(n_heads)[:, None] // group == jnp.arange(n_kv)[None, :]).astype(q.dtype)
    q4 = q.reshape(n, n_heads, 1, HEAD_DIM) * onehot[None, :, :, None]
    return q4.reshape(n, n_heads, n_kv * HEAD_DIM)


def kernel(x_prompt, x_sample, cache_a_k, cache_a_v, state_b_k, state_b_v, page_table, g_a, w_in_a, qn_a, kn_a, w_out_a, g_kv, w_kv, kn_b, g_b, w_in_b, qn_b, sinks_b, w_out_b):
    _, s, d = x_prompt.shape
    n, t, _ = x_sample.shape
    assert t == 1 and g_a.shape[0] == 1 and g_b.shape[0] == 1
    n_heads = w_out_a.shape[1] // HEAD_DIM
    b_heads = w_out_b.shape[1] // HEAD_DIM
    past = page_table.shape[1] * PAGE_SIZE
    n_pool = cache_a_k.shape[1]
    a_kvw = A_KV_HEADS * HEAD_DIM
    b_kvw = B_KV_HEADS * HEAD_DIM

    cos_p, sin_p = _rope_tables(jnp.arange(s, dtype=jnp.int32))
    cos_s, sin_s = _rope_tables(jnp.full((n,), past, jnp.int32))
    lane = jnp.arange(LANES)
    b2 = (lane[:, None] // HEAD_DIM == lane[None, :] // HEAD_DIM).astype(BF16)
    gain_a = jnp.concatenate([jnp.tile(qn_a[0], n_heads), jnp.tile(kn_a[0], A_KV_HEADS)])[None, :]
    w_a = w_in_a[0].astype(BF16)
    wo_a = w_out_a[0].astype(BF16)
    w_kvb = w_kv.astype(BF16)
    w_b = w_in_b[0].astype(BF16)
    wo_b = w_out_b[0].astype(BF16)
    knb = jnp.tile(kn_b, HEADS_PER_VREG)[None, :]
    qnb = jnp.tile(qn_b[0], HEADS_PER_VREG)[None, :]
    post_consts = (wo_a, g_kv[None, :], w_kvb, knb, g_b[0][None, :], w_b, qnb)

    xp = x_prompt[0]
    qa, ka, va, k_p, v_p, gate_p = _proj_a_prompt(xp, w_a, g_a[0][None, :], gain_a, cos_p, sin_p, b2,
                                                  n_heads, A_KV_HEADS)
    o_p = _moba_prompt(qa, ka, va)
    h_p, kb_p, vb_p, qb_p, gateb_p = _post_a(o_p, gate_p, xp, *post_consts, cos_p, sin_p, b2, MOBA_BLOCK)
    y_p = _swa_prompt(qb_p, kb_p, vb_p, sinks_b[0], gateb_p, h_p, wo_b)

    xs = x_sample[:, 0]
    q_s, k_s, v_s, gate_s = _proj_a_sample(xs, w_a, g_a[0][None, :], gain_a, cos_s, sin_s, b2,
                                           n_heads, A_KV_HEADS)
    feature_major = lambda a: jnp.moveaxis(a, -3, -1).reshape(a.shape[:-3] + (a.shape[-2] * HEAD_DIM, a.shape[-3]))
    o_s = _moba_sample(page_table, _slot_rows(q_s, n_heads, A_KV_HEADS), k_s[:, None, :], v_s[:, None, :],
                       feature_major(cache_a_k[0]), feature_major(cache_a_v[0]))
    h_s, kb_s, vb_s, qb_s, gateb_s = _post_a(o_s.reshape(n, n_heads * HEAD_DIM), gate_s, xs, *post_consts,
                                             cos_s, sin_s, b2, n)
    ob_s, win_skt, win_svt = _swa_sample(_slot_rows(qb_s, b_heads, B_KV_HEADS),
                                         feature_major(state_b_k), feature_major(state_b_v),
                                         kb_s, vb_s, sinks_b[0][:, None])
    y_s = _gated_out(ob_s.reshape(n, b_heads * HEAD_DIM), gateb_s, h_s, wo_b)
    token_major = lambda a: jnp.moveaxis(a.reshape(n, B_KV_HEADS, HEAD_DIM, WINDOW), -1, 1)
    win_sk, win_sv = token_major(win_skt), token_major(win_svt)

    return (y_p[None], y_s[:, None, :],
            k_p.reshape(1, 1, s, A_KV_HEADS, HEAD_DIM), v_p.reshape(1, 1, s, A_KV_HEADS, HEAD_DIM),
            kb_p[s - WINDOW:].reshape(1, WINDOW, B_KV_HEADS, HEAD_DIM),
            vb_p[s - WINDOW:].reshape(1, WINDOW, B_KV_HEADS, HEAD_DIM),
            k_s.reshape(1, n, 1, A_KV_HEADS, HEAD_DIM), v_s.reshape(1, n, 1, A_KV_HEADS, HEAD_DIM),
            win_sk, win_sv)
```

```python
import functools

import jax
import jax.numpy as jnp
from jax import lax
from jax.experimental import pallas as pl
from jax.experimental.pallas import tpu as pltpu

F32 = jnp.float32
BF16 = jnp.bfloat16

HEAD_DIM = 64
HALF = HEAD_DIM // 2
A_KV_HEADS = 4
B_KV_HEADS = 2
MOBA_BLOCK = 256
MOBA_TOPK = 3
PAGE_SIZE = 128
WINDOW = 128
ROPE_THETA = 10000.0
EPS = 1e-6
NEG_INF = -1e30
SCALE = HEAD_DIM ** -0.5

LANES = 128
HEADS_PER_VREG = LANES // HEAD_DIM
VMEM_LIMIT = 56 * 1024 * 1024

PAGES_PER_CHUNK = 32
RING = 4
LOOKAHEAD = RING - 1


def _nt(a, b):
    return lax.dot_general(a, b, (((1,), (1,)), ((), ())), preferred_element_type=F32)


def _split(x):
    hi = x.astype(BF16)
    lo = (x - hi.astype(F32)).astype(BF16)
    return hi, lo


def _nt3(a, b_split):
    a_hi, a_lo = _split(a)
    b_hi, b_lo = b_split
    return _nt(a_hi, b_hi) + _nt(a_hi, b_lo) + _nt(a_lo, b_hi)


def _silu(g):
    return g * (1.0 / (1.0 + jnp.exp(-g)))


def _rms_scale(x):
    return lax.rsqrt(jnp.mean(x * x, axis=-1, keepdims=True) + EPS)


def _head_norm_rope(zz, gain, cos, sin, b2, first_half):
    ss = jnp.dot((zz * zz).astype(BF16), b2, preferred_element_type=F32)
    zh = zz * lax.rsqrt(ss * (1.0 / HEAD_DIM) + EPS) * gain
    partner = jnp.where(first_half, pltpu.roll(zh, LANES - HALF, 1), pltpu.roll(zh, HALF, 1))
    return zh * cos + partner * sin


def _top3_keep(g, valid, lane):
    cand = jnp.where(valid, g, NEG_INF)
    sel = jnp.zeros(g.shape, jnp.bool_)
    lane_f = lane.astype(F32)
    for _ in range(MOBA_TOPK):
        mx = jnp.max(cand, axis=1, keepdims=True)
        first = jnp.min(jnp.where(cand == mx, lane_f, 2.0 * LANES), axis=1, keepdims=True)
        pick = lane_f == first
        sel = sel | pick
        cand = jnp.where(pick, -jnp.inf, cand)
    return sel & valid


def _proj_a_prompt_kernel(x_ref, w_ref, g_ref, gain_ref, cos_ref, sin_ref, b2_ref,
                          qa_ref, ka_ref, va_ref, k_ref, v_ref, gate_ref, kmt_ref,
                          *, n_heads, n_kv):
    i = pl.program_id(0)
    tm = x_ref.shape[0]
    a_width = n_heads * HEAD_DIM
    kv_width = n_kv * HEAD_DIM

    @pl.when(i == 0)
    def _():
        kmt_ref[...] = jnp.zeros(kmt_ref.shape, F32)

    x = x_ref[...]
    xb = (x * _rms_scale(x) * g_ref[...]).astype(BF16)
    z = jnp.dot(xb, w_ref[...], preferred_element_type=F32)

    lane = lax.broadcasted_iota(jnp.int32, (tm, LANES), 1)
    first_half = (lane % HEAD_DIM) < HALF
    low = lane < HEAD_DIM
    cos = cos_ref[...]
    sin = sin_ref[...]
    b2 = b2_ref[...]
    blk_lane = lane - HEAD_DIM
    past = (lane >= HEAD_DIM) & (blk_lane < i)
    own = (blk_lane == i) | (lane == LANES - 1)

    group = n_heads // n_kv
    kmt_split = [_split(kmt_ref[kvh]) for kvh in range(n_kv)]
    for c in range(a_width // LANES):
        ro = _head_norm_rope(z[:, c * LANES:(c + 1) * LANES], gain_ref[:, c * LANES:(c + 1) * LANES],
                             cos, sin, b2, first_half)
        g2 = _nt3(ro, kmt_split[(c * HEADS_PER_VREG) // group])
        qs = ro * SCALE
        qs_r = pltpu.roll(qs, HEAD_DIM, 1)
        for e in range(HEADS_PER_VREG):
            keep = _top3_keep(g2[:, e * LANES:(e + 1) * LANES], past, lane) | own
            bias = jnp.where(keep, 0.0, NEG_INF)
            qa = jnp.where(low, qs if e == 0 else qs_r, bias)
            qa_ref[c * HEADS_PER_VREG + e] = qa.astype(BF16)

    onehot = jnp.where(own, 1.0, 0.0)
    for c in range(kv_width // LANES):
        col = a_width + c * LANES
        ro = _head_norm_rope(z[:, col:col + LANES], gain_ref[:, col:col + LANES], cos, sin, b2, first_half)
        k_ref[:, c * LANES:(c + 1) * LANES] = ro
        ro_r = pltpu.roll(ro, HEAD_DIM, 1)
        cs = jnp.sum(ro, axis=0, keepdims=True) * (1.0 / MOBA_BLOCK)
        cs_r = jnp.sum(ro_r, axis=0, keepdims=True) * (1.0 / MOBA_BLOCK)
        low1 = low[0:1]
        for e in range(HEADS_PER_VREG):
            kvh = c * HEADS_PER_VREG + e
            ka_ref[kvh] = jnp.where(low, ro if e == 0 else ro_r, onehot).astype(BF16)
            kmt_ref[kvh, pl.ds(HEAD_DIM + i, 1), :] = jnp.where(low1, cs if e == 0 else cs_r, 0.0)
            kmt_ref[kvh, pl.ds(LANES + HEAD_DIM + i, 1), :] = jnp.where(low1, 0.0, cs_r if e == 0 else cs)

    ones_col = jnp.where(lane == HEAD_DIM, 1.0, 0.0)
    v0 = a_width + kv_width
    v_ref[...] = z[:, v0:v0 + kv_width]
    for c in range(kv_width // LANES):
        vv = z[:, v0 + c * LANES:v0 + (c + 1) * LANES]
        vv_r = pltpu.roll(vv, HEAD_DIM, 1)
        for e in range(HEADS_PER_VREG):
            va_ref[c * HEADS_PER_VREG + e] = jnp.where(low, vv if e == 0 else vv_r, ones_col).astype(BF16)

    gate_ref[...] = z[:, v0 + kv_width:]


def _proj_a_sample_kernel(x_ref, w_ref, g_ref, gain_ref, cos_ref, sin_ref, b2_ref,
                          q_ref, k_ref, v_ref, gate_ref, *, n_heads, n_kv):
    tm = x_ref.shape[0]
    a_width = n_heads * HEAD_DIM
    kv_width = n_kv * HEAD_DIM
    x = x_ref[...]
    xb = (x * _rms_scale(x) * g_ref[...]).astype(BF16)
    z = jnp.dot(xb, w_ref[...], preferred_element_type=F32)
    lane = lax.broadcasted_iota(jnp.int32, (tm, LANES), 1)
    first_half = (lane % HEAD_DIM) < HALF
    cos = cos_ref[...]
    sin = sin_ref[...]
    b2 = b2_ref[...]
    for c in range((a_width + kv_width) // LANES):
        ro = _head_norm_rope(z[:, c * LANES:(c + 1) * LANES], gain_ref[:, c * LANES:(c + 1) * LANES],
                             cos, sin, b2, first_half)
        if c * LANES < a_width:
            q_ref[:, c * LANES:(c + 1) * LANES] = ro
        else:
            k_ref[:, c * LANES - a_width:(c + 1) * LANES - a_width] = ro
    v0 = a_width + kv_width
    v_ref[...] = z[:, v0:v0 + kv_width]
    gate_ref[...] = z[:, v0 + kv_width:]


def _const_spec(shape):
    return pl.BlockSpec(shape, lambda *_: (0,) * len(shape))


def _proj_a_prompt(x, w, g, gain, cos, sin, b2, n_heads, n_kv):
    s, d = x.shape
    tm = MOBA_BLOCK
    n_in = w.shape[1]
    a_width = n_heads * HEAD_DIM
    kv_width = n_kv * HEAD_DIM
    row = lambda width: pl.BlockSpec((tm, width), lambda i: (i, 0))
    head_plane = lambda n: pl.BlockSpec((n, tm, LANES), lambda i: (0, i, 0))
    return pl.pallas_call(
        functools.partial(_proj_a_prompt_kernel, n_heads=n_heads, n_kv=n_kv),
        grid=(s // tm,),
        in_specs=[row(d), _const_spec((d, n_in)), _const_spec((1, d)), _const_spec((1, a_width + kv_width)),
                  row(LANES), row(LANES), _const_spec((LANES, LANES))],
        out_specs=[head_plane(n_heads), head_plane(n_kv), head_plane(n_kv),
                   row(kv_width), row(kv_width), row(a_width)],
        out_shape=[jax.ShapeDtypeStruct((n_heads, s, LANES), BF16),
                   jax.ShapeDtypeStruct((n_kv, s, LANES), BF16),
                   jax.ShapeDtypeStruct((n_kv, s, LANES), BF16),
                   jax.ShapeDtypeStruct((s, kv_width), F32),
                   jax.ShapeDtypeStruct((s, kv_width), F32),
                   jax.ShapeDtypeStruct((s, a_width), F32)],
        scratch_shapes=[pltpu.VMEM((n_kv, 2 * LANES, LANES), F32)],
        compiler_params=pltpu.CompilerParams(dimension_semantics=("arbitrary",),
                                             vmem_limit_bytes=VMEM_LIMIT),
        name="proj_a_prompt",
    )(x, w, g, gain, cos, sin, b2)


def _proj_a_sample(x, w, g, gain, cos, sin, b2, n_heads, n_kv):
    n, d = x.shape
    a_width = n_heads * HEAD_DIM
    kv_width = n_kv * HEAD_DIM
    full = lambda a: _const_spec(a.shape)
    out_shape = [jax.ShapeDtypeStruct((n, a_width), F32), jax.ShapeDtypeStruct((n, kv_width), F32),
                 jax.ShapeDtypeStruct((n, kv_width), F32), jax.ShapeDtypeStruct((n, a_width), F32)]
    return pl.pallas_call(
        functools.partial(_proj_a_sample_kernel, n_heads=n_heads, n_kv=n_kv),
        grid=(1,),
        in_specs=[full(a) for a in (x, w, g, gain, cos, sin, b2)],
        out_specs=[_const_spec(o.shape) for o in out_shape],
        out_shape=out_shape,
        compiler_params=pltpu.CompilerParams(vmem_limit_bytes=VMEM_LIMIT),
        name="proj_a_sample",
    )(x, w, g, gain, cos, sin, b2)


def _moba_prompt_kernel(q_ref, k_ref, v_ref, o_ref, m_ref, acc_ref, q2_ref):
    i = pl.program_id(1)
    group, tq, _ = q_ref.shape
    rows = group * tq
    span = 2 * MOBA_BLOCK
    n_full = i // 2
    q = q_ref[...].reshape(rows, LANES)

    def scores(qq, t):
        k2 = k_ref[pl.ds(pl.multiple_of(t * span, span), span), :]
        return _nt(qq, k2)

    def fold_max(s):
        return jnp.maximum(jnp.maximum(s[:, 0:LANES], s[:, LANES:2 * LANES]),
                           jnp.maximum(s[:, 2 * LANES:3 * LANES], s[:, 3 * LANES:]))

    tok = lax.broadcasted_iota(jnp.int32, (rows, span), 0) % tq + i * tq
    key = lax.broadcasted_iota(jnp.int32, (rows, span), 1) + n_full * span
    causal = key <= tok

    m_ref[...] = fold_max(jnp.where(causal, scores(q, n_full), NEG_INF))

    def max_body(t, carry):
        m_ref[...] = jnp.maximum(m_ref[...], fold_max(scores(q, t)))
        return carry

    lax.fori_loop(0, n_full, max_body, 0)

    m = jnp.max(m_ref[...], axis=1, keepdims=True).astype(BF16)
    lane = lax.broadcasted_iota(jnp.int32, (rows, LANES), 1)
    q2_ref[...] = jnp.where(lane == LANES - 1, -m, q)
    p = jnp.exp(jnp.where(causal, scores(q2_ref[...], n_full), NEG_INF))
    v_last = v_ref[pl.ds(pl.multiple_of(n_full * span, span), span), :]
    acc_ref[...] = jnp.dot(p.astype(BF16), v_last, preferred_element_type=F32)

    def pv_body(t, carry):
        v2 = v_ref[pl.ds(pl.multiple_of(t * span, span), span), :]
        p = jnp.exp(scores(q2_ref[...], t))
        acc_ref[...] += jnp.dot(p.astype(BF16), v2, preferred_element_type=F32)
        return carry

    lax.fori_loop(0, n_full, pv_body, 0)

    acc = acc_ref[...]
    o = acc[:, :HEAD_DIM] / acc[:, HEAD_DIM:HEAD_DIM + 1]
    for g in range(group):
        o_ref[:, g * HEAD_DIM:(g + 1) * HEAD_DIM] = o[g * tq:(g + 1) * tq]


def _moba_prompt(qa, ka, va):
    n_heads, s, _ = qa.shape
    n_kv = ka.shape[0]
    group = n_heads // n_kv
    tq = MOBA_BLOCK
    return pl.pallas_call(
        _moba_prompt_kernel,
        grid=(n_kv, s // tq),
        in_specs=[pl.BlockSpec((group, tq, LANES), lambda c, i: (c, i, 0)),
                  pl.BlockSpec((None, s, LANES), lambda c, i: (c, 0, 0)),
                  pl.BlockSpec((None, s, LANES), lambda c, i: (c, 0, 0))],
        out_specs=pl.BlockSpec((tq, group * HEAD_DIM), lambda c, i: (i, c)),
        out_shape=jax.ShapeDtypeStruct((s, n_heads * HEAD_DIM), F32),
        scratch_shapes=[pltpu.VMEM((group * tq, LANES), F32), pltpu.VMEM((group * tq, LANES), F32),
                        pltpu.VMEM((group * tq, LANES), BF16)],
        compiler_params=pltpu.CompilerParams(dimension_semantics=("arbitrary", "arbitrary"),
                                             vmem_limit_bytes=VMEM_LIMIT),
        name="moba_prompt",
    )(qa, ka, va)


def _post_a_kernel(o_ref, gate_ref, x_ref, wo_ref, gkv_ref, wkv_ref, knb_ref, gb_ref, wb_ref, qnb_ref,
                   cos_ref, sin_ref, b2_ref, h_ref, kb_ref, vb_ref, qb_ref, gateb_ref):
    tm = x_ref.shape[0]
    b_width = qb_ref.shape[1]
    h = x_ref[...] + jnp.dot((o_ref[...] * _silu(gate_ref[...])).astype(BF16), wo_ref[...],
                             preferred_element_type=F32)
    h_ref[...] = h
    hn = h * _rms_scale(h)
    lane = lax.broadcasted_iota(jnp.int32, (tm, LANES), 1)
    first_half = (lane % HEAD_DIM) < HALF
    cos = cos_ref[...]
    sin = sin_ref[...]
    b2 = b2_ref[...]

    zkv = jnp.dot((hn * gkv_ref[...]).astype(BF16), wkv_ref[...], preferred_element_type=F32)
    kvw = kb_ref.shape[1]
    for c in range(kvw // LANES):
        kb_ref[:, c * LANES:(c + 1) * LANES] = _head_norm_rope(
            zkv[:, c * LANES:(c + 1) * LANES], knb_ref[...], cos, sin, b2, first_half)
    vb_ref[...] = zkv[:, kvw:]

    zb = jnp.dot((hn * gb_ref[...]).astype(BF16), wb_ref[...], preferred_element_type=F32)
    for c in range(b_width // LANES):
        ro = _head_norm_rope(zb[:, c * LANES:(c + 1) * LANES], qnb_ref[...], cos, sin, b2, first_half)
        qb_ref[:, c * LANES:(c + 1) * LANES] = (ro * SCALE).astype(BF16)
    gateb_ref[...] = zb[:, b_width:]


def _post_a(o, gate, x, wo, gkv, wkv, knb, gb, wb, qnb, cos, sin, b2, tm):
    s, d = x.shape
    kvw = wkv.shape[1] // 2
    b_width = wb.shape[1] // 2
    row = lambda width: pl.BlockSpec((tm, width), lambda i: (i, 0))
    consts = (wo, gkv, wkv, knb, gb, wb, qnb)
    return pl.pallas_call(
        _post_a_kernel,
        grid=(s // tm,),
        in_specs=[row(o.shape[1]), row(gate.shape[1]), row(d)] + [_const_spec(a.shape) for a in consts]
                 + [row(LANES), row(LANES), _const_spec(b2.shape)],
        out_specs=[row(d), row(kvw), row(kvw), row(b_width), row(b_width)],
        out_shape=[jax.ShapeDtypeStruct((s, d), F32), jax.ShapeDtypeStruct((s, kvw), F32),
                   jax.ShapeDtypeStruct((s, kvw), F32), jax.ShapeDtypeStruct((s, b_width), BF16),
                   jax.ShapeDtypeStruct((s, b_width), F32)],
        compiler_params=pltpu.CompilerParams(dimension_semantics=("parallel",),
                                             vmem_limit_bytes=VMEM_LIMIT),
        name="post_a",
    )(o, gate, x, *consts, cos, sin, b2)


def _swa_prompt_kernel(q_ref, kc_ref, kp_ref, vc_ref, vp_ref, sink_ref, gate_ref, h_ref, wo_ref,
                       y_ref, o_scr):
    n = pl.program_id(0)
    w = q_ref.shape[0]
    n_heads = q_ref.shape[1] // HEAD_DIM
    n_kv = kc_ref.shape[1] // HEAD_DIM
    group = n_heads // n_kv
    kk = jnp.concatenate([kp_ref[...], kc_ref[...]], axis=0).astype(BF16)
    vv = jnp.concatenate([vp_ref[...], vc_ref[...]], axis=0).astype(BF16)
    rows = group * w
    t = lax.broadcasted_iota(jnp.int32, (rows, 2 * w), 0) % w
    j = lax.broadcasted_iota(jnp.int32, (rows, 2 * w), 1)
    mask = (j >= t) & (j <= t + w) & ((n > 0) | (j >= w))
    lane = lax.broadcasted_iota(jnp.int32, (w, LANES), 1)

    for c in range(n_kv):
        in_slot = (lane // HEAD_DIM) == c
        parts = []
        sinks = []
        for g in range(group):
            hd = c * group + g
            vreg = q_ref[:, (hd // HEADS_PER_VREG) * LANES:(hd // HEADS_PER_VREG + 1) * LANES].astype(F32)
            if hd % HEADS_PER_VREG != c:
                vreg = pltpu.roll(vreg, HEAD_DIM, 1)
            parts.append(jnp.where(in_slot, vreg, 0.0).astype(BF16))
            sinks.append(jnp.full((w, 1), sink_ref[hd], F32))
        qs = jnp.concatenate(parts, axis=0)
        sk = jnp.concatenate(sinks, axis=0)
        s = jnp.where(mask, _nt(qs, kk), NEG_INF)
        m = jnp.maximum(jnp.max(s, axis=1, keepdims=True), sk)
        e = jnp.exp(s - m)
        inv = 1.0 / (jnp.sum(e, axis=1, keepdims=True) + jnp.exp(sk - m))
        o = jnp.dot(e.astype(BF16), vv, preferred_element_type=F32) * inv
        for g in range(0, group, HEADS_PER_VREG):
            hd = c * group + g
            even = o[g * w:(g + 1) * w]
            odd = o[(g + 1) * w:(g + 2) * w]
            if c == 0:
                odd = pltpu.roll(odd, HEAD_DIM, 1)
            else:
                even = pltpu.roll(even, HEAD_DIM, 1)
            dst = (hd // HEADS_PER_VREG) * LANES
            o_scr[:, dst:dst + LANES] = jnp.where(lane < HEAD_DIM, even, odd)

    y_ref[...] = h_ref[...] + jnp.dot((o_scr[...] * _silu(gate_ref[...])).astype(BF16), wo_ref[...],
                                      preferred_element_type=F32)


def _swa_prompt(qb, kb, vb, sinks, gateb, h, wo):
    s, b_width = qb.shape
    kvw = kb.shape[1]
    d = h.shape[1]
    w = WINDOW
    cur = lambda width: pl.BlockSpec((w, width), lambda n: (n, 0))
    prev = lambda width: pl.BlockSpec((w, width), lambda n: (jnp.maximum(n - 1, 0), 0))
    return pl.pallas_call(
        _swa_prompt_kernel,
        grid=(s // w,),
        in_specs=[cur(b_width), cur(kvw), prev(kvw), cur(kvw), prev(kvw),
                  pl.BlockSpec(memory_space=pltpu.SMEM), cur(b_width), cur(d), _const_spec(wo.shape)],
        out_specs=cur(d),
        out_shape=jax.ShapeDtypeStruct((s, d), F32),
        scratch_shapes=[pltpu.VMEM((w, b_width), F32)],
        compiler_params=pltpu.CompilerParams(dimension_semantics=("parallel",),
                                             vmem_limit_bytes=VMEM_LIMIT),
        name="swa_prompt",
    )(qb, kb, kb, vb, vb, sinks, gateb, h, wo)


def _moba_sample_kernel(pt_ref, wq_ref, knew_ref, vnew_ref, expand_ref, kc_ref, vc_ref,
                        o_ref, buf, sem, s_scr, p_scr, ksum_scr):
    b = pl.program_id(0)
    n_samples = pl.num_programs(0)
    n_heads = wq_ref.shape[1]
    kvw = wq_ref.shape[2]
    n_blocks = s_scr.shape[1] // MOBA_BLOCK
    blocks_per_chunk = PAGES_PER_CHUNK * PAGE_SIZE // MOBA_BLOCK
    assert MOBA_BLOCK == 2 * PAGE_SIZE and n_blocks <= LANES
    jobs = 2 * RING

    def page_copy(sample, job, p):
        cache = kc_ref if job < RING else vc_ref
        slot = job % RING
        page = pt_ref[sample, (job % RING) * PAGES_PER_CHUNK + p]
        return pltpu.make_async_copy(cache.at[page], buf.at[slot, p], sem.at[slot])

    def issue(sample, job):
        def body(p, carry):
            page_copy(sample, job, p).start()
            return carry
        lax.fori_loop(0, PAGES_PER_CHUNK, body, 0)

    def wait(sample, job):
        def body(p, carry):
            page_copy(sample, job, p).wait()
            return carry
        lax.fori_loop(0, PAGES_PER_CHUNK, body, 0)

    @pl.when(b == 0)
    def _():
        ksum_scr[...] = jnp.zeros(ksum_scr.shape, F32)
        for job in range(LOOKAHEAD):
            issue(b, job)

    def prefetch(job):
        nxt = job + LOOKAHEAD
        if nxt < jobs:
            issue(b, nxt)
        else:
            @pl.when(b + 1 < n_samples)
            def _():
                issue(b + 1, nxt - jobs)

    wq = wq_ref[0]
    wqs = (wq * SCALE).astype(BF16)

    lane_k = lax.broadcasted_iota(jnp.int32, (kvw, LANES), 1)
    for job in range(RING):
        wait(b, job)
        prefetch(job)

        def k_body(r, carry, job=job):
            kt0 = buf[job, 2 * r]
            kt1 = buf[job, 2 * r + 1]
            blk = job * blocks_per_chunk + r
            kt = jnp.concatenate([kt0, kt1], axis=1).astype(BF16)
            off = pl.multiple_of(blk * MOBA_BLOCK, MOBA_BLOCK)
            s_scr[:, pl.ds(off, MOBA_BLOCK)] = jnp.dot(wqs, kt, preferred_element_type=F32)
            col = jnp.sum(kt0 + kt1, axis=1, keepdims=True)
            ksum_scr[...] = jnp.where(lane_k == blk, col, ksum_scr[...])
            return carry

        lax.fori_loop(0, blocks_per_chunk, k_body, 0, unroll=4)

    kmean = ksum_scr[...] * (1.0 / MOBA_BLOCK)
    wq_hi, wq_lo = _split(wq)
    km_hi, km_lo = _split(kmean)
    dot = functools.partial(jnp.dot, preferred_element_type=F32)
    gate = dot(wq_hi, km_hi) + dot(wq_hi, km_lo) + dot(wq_lo, km_hi)
    lane = lax.broadcasted_iota(jnp.int32, gate.shape, 1)
    keep = _top3_keep(gate, lane < n_blocks, lane)
    bias = jnp.where(keep, 0.0, NEG_INF).astype(BF16)
    s = s_scr[...] + dot(bias, expand_ref[...])
    s_self = jnp.sum(wq * knew_ref[0], axis=1, keepdims=True) * SCALE
    m = jnp.maximum(jnp.max(s, axis=1, keepdims=True), s_self)
    p = jnp.exp(s - m)
    p_self = jnp.exp(s_self - m)
    denom = jnp.sum(p, axis=1, keepdims=True) + p_self
    p_scr[...] = p.astype(BF16)

    acc = jnp.zeros((n_heads, kvw), F32)
    for job in range(RING, jobs):
        wait(b, job)
        prefetch(job)

        def v_body(r, acc, job=job):
            vt = jnp.concatenate([buf[job % RING, 2 * r], buf[job % RING, 2 * r + 1]], axis=1)
            off = pl.multiple_of(((job - RING) * blocks_per_chunk + r) * MOBA_BLOCK, MOBA_BLOCK)
            return acc + _nt(p_scr[:, pl.ds(off, MOBA_BLOCK)], vt.astype(BF16))

        acc = lax.fori_loop(0, blocks_per_chunk, v_body, acc, unroll=4)

    o = (acc + p_self * vnew_ref[0]) / denom
    head = lax.broadcasted_iota(jnp.int32, (n_heads, HEAD_DIM), 0)
    group = n_heads // (kvw // HEAD_DIM)
    out = jnp.zeros((n_heads, HEAD_DIM), F32)
    for c in range(kvw // HEAD_DIM):
        out = jnp.where(head // group == c, o[:, c * HEAD_DIM:(c + 1) * HEAD_DIM], out)
    o_ref[0] = out


def _moba_sample(page_table, wq, knew, vnew, cache_kt, cache_vt):
    n, n_heads, kvw = wq.shape
    n_pages = page_table.shape[1]
    assert n_pages == RING * PAGES_PER_CHUNK and cache_kt.shape[1:] == (kvw, PAGE_SIZE)
    n_keys = n_pages * PAGE_SIZE
    expand = (jnp.arange(LANES)[:, None] == jnp.arange(n_keys)[None, :] // MOBA_BLOCK).astype(BF16)
    grid_spec = pltpu.PrefetchScalarGridSpec(
        num_scalar_prefetch=1,
        grid=(n,),
        in_specs=[pl.BlockSpec((1, n_heads, kvw), lambda b, pt: (b, 0, 0)),
                  pl.BlockSpec((1, 1, kvw), lambda b, pt: (b, 0, 0)),
                  pl.BlockSpec((1, 1, kvw), lambda b, pt: (b, 0, 0)),
                  pl.BlockSpec(expand.shape, lambda b, pt: (0, 0)),
                  pl.BlockSpec(memory_space=pl.ANY),
                  pl.BlockSpec(memory_space=pl.ANY)],
        out_specs=pl.BlockSpec((1, n_heads, HEAD_DIM), lambda b, pt: (b, 0, 0)),
        scratch_shapes=[pltpu.VMEM((RING, PAGES_PER_CHUNK, kvw, PAGE_SIZE), F32),
                        pltpu.SemaphoreType.DMA((RING,)),
                        pltpu.VMEM((n_heads, n_keys), F32),
                        pltpu.VMEM((n_heads, n_keys), BF16),
                        pltpu.VMEM((kvw, LANES), F32)],
    )
    return pl.pallas_call(
        _moba_sample_kernel,
        grid_spec=grid_spec,
        out_shape=jax.ShapeDtypeStruct((n, n_heads, HEAD_DIM), F32),
        compiler_params=pltpu.CompilerParams(dimension_semantics=("arbitrary",),
                                             vmem_limit_bytes=VMEM_LIMIT),
        name="moba_sample",
    )(page_table, wq, knew, vnew, expand, cache_kt, cache_vt)


def _swa_sample_kernel(wq_ref, sk_ref, sv_ref, knew_ref, vnew_ref, knewt_ref, vnewt_ref, sink_ref,
                       o_ref, wk_ref, wv_ref):
    nb, n_heads, kvw = wq_ref.shape
    w = sk_ref.shape[2]
    group = n_heads // (kvw // HEAD_DIM)
    head = lax.broadcasted_iota(jnp.int32, (n_heads, HEAD_DIM), 0)
    pos = lax.broadcasted_iota(jnp.int32, (kvw, w), 1)
    col_id = lax.broadcasted_iota(jnp.int32, knewt_ref.shape, 1)
    sink = sink_ref[...]
    for i in range(nb):
        wq = wq_ref[i]
        kt = sk_ref[i]
        vt = sv_ref[i]
        knew = knew_ref[i]
        vnew = vnew_ref[i]
        s = jnp.dot(wq, kt.astype(BF16), preferred_element_type=F32)
        s_new = jnp.sum(wq.astype(F32) * knew, axis=1, keepdims=True)
        m = jnp.maximum(jnp.maximum(jnp.max(s, axis=1, keepdims=True), s_new), sink)
        e = jnp.exp(s - m)
        e_new = jnp.exp(s_new - m)
        denom = jnp.sum(e, axis=1, keepdims=True) + e_new + jnp.exp(sink - m)
        o = _nt((e / denom).astype(BF16), vt.astype(BF16)) + (e_new / denom) * vnew
        out = jnp.zeros((n_heads, HEAD_DIM), F32)
        for c in range(kvw // HEAD_DIM):
            out = jnp.where(head // group == c, o[:, c * HEAD_DIM:(c + 1) * HEAD_DIM], out)
        o_ref[i] = out
        mine = col_id == pl.program_id(0) * nb + i
        knew_col = jnp.sum(jnp.where(mine, knewt_ref[...], 0.0), axis=1, keepdims=True)
        vnew_col = jnp.sum(jnp.where(mine, vnewt_ref[...], 0.0), axis=1, keepdims=True)
        wk_ref[i] = jnp.where(pos == w - 1, knew_col, pltpu.roll(kt, w - 1, 1))
        wv_ref[i] = jnp.where(pos == w - 1, vnew_col, pltpu.roll(vt, w - 1, 1))


def _swa_sample(wq, state_kt, state_vt, knew, vnew, sinks):
    n, n_heads, kvw = wq.shape
    w = state_kt.shape[2]
    nb = 8
    blk = lambda shape: pl.BlockSpec((nb,) + shape, lambda i: (i,) + (0,) * len(shape))
    knew_t = knew.T
    vnew_t = vnew.T
    return pl.pallas_call(
        _swa_sample_kernel,
        grid=(n // nb,),
        in_specs=[blk((n_heads, kvw)), blk((kvw, w)), blk((kvw, w)), blk((1, kvw)), blk((1, kvw)),
                  _const_spec(knew_t.shape), _const_spec(vnew_t.shape), _const_spec(sinks.shape)],
        out_specs=[blk((n_heads, HEAD_DIM)), blk((kvw, w)), blk((kvw, w))],
        out_shape=[jax.ShapeDtypeStruct((n, n_heads, HEAD_DIM), F32),
                   jax.ShapeDtypeStruct((n, kvw, w), F32), jax.ShapeDtypeStruct((n, kvw, w), F32)],
        compiler_params=pltpu.CompilerParams(dimension_semantics=("parallel",),
                                             vmem_limit_bytes=VMEM_LIMIT),
        name="swa_sample",
    )(wq, state_kt, state_vt, knew[:, None, :], vnew[:, None, :], knew_t, vnew_t, sinks)


def _gated_out_kernel(o_ref, gate_ref, h_ref, wo_ref, y_ref):
    y_ref[...] = h_ref[...] + jnp.dot((o_ref[...] * _silu(gate_ref[...])).astype(BF16), wo_ref[...],
                                      preferred_element_type=F32)


def _gated_out(o, gate, h, wo):
    return pl.pallas_call(
        _gated_out_kernel,
        grid=(1,),
        in_specs=[_const_spec(a.shape) for a in (o, gate, h, wo)],
        out_specs=_const_spec(h.shape),
        out_shape=jax.ShapeDtypeStruct(h.shape, F32),
        compiler_params=pltpu.CompilerParams(vmem_limit_bytes=VMEM_LIMIT),
        name="gated_out",
    )(o, gate, h, wo)


def _rope_tables(pos):
    inv = 1.0 / (ROPE_THETA ** (jnp.arange(HALF, dtype=F32) / HALF))
    ang = pos.astype(F32)[:, None] * inv[None, :]
    c, s = jnp.cos(ang), jnp.sin(ang)
    return jnp.tile(c, (1, LANES // HALF)), jnp.tile(jnp.concatenate([-s, s], axis=1), (1, HEADS_PER_VREG))


def _slot_rows(q, n_heads, n_kv):
    n = q.shape[0]
    group = n_heads // n_kv
    onehot = (jnp.arange(n_heads)[:, None] // group == jnp.arange(n_kv)[None, :]).astype(q.dtype)
    q4 = q.reshape(n, n_heads, 1, HEAD_DIM) * onehot[None, :, :, None]
    return q4.reshape(n, n_heads, n_kv * HEAD_DIM)


def kernel(x_prompt, x_sample, cache_a_k, cache_a_v, state_b_k, state_b_v, page_table, g_a, w_in_a, qn_a, kn_a, w_out_a, g_kv, w_kv, kn_b, g_b, w_in_b, qn_b, sinks_b, w_out_b):
    _, s, d = x_prompt.shape
    n, t, _ = x_sample.shape
    assert t == 1 and g_a.shape[0] == 1 and g_b.shape[0] == 1
    n_heads = w_out_a.shape[1] // HEAD_DIM
    b_heads = w_out_b.shape[1] // HEAD_DIM
    past = page_table.shape[1] * PAGE_SIZE
    n_pool = cache_a_k.shape[1]
    a_kvw = A_KV_HEADS * HEAD_DIM
    b_kvw = B_KV_HEADS * HEAD_DIM

    cos_p, sin_p = _rope_tables(jnp.arange(s, dtype=jnp.int32))
    cos_s, sin_s = _rope_tables(jnp.full((n,), past, jnp.int32))
    lane = jnp.arange(LANES)
    b2 = (lane[:, None] // HEAD_DIM == lane[None, :] // HEAD_DIM).astype(BF16)
    gain_a = jnp.concatenate([jnp.tile(qn_a[0], n_heads), jnp.tile(kn_a[0], A_KV_HEADS)])[None, :]
    w_a = w_in_a[0].astype(BF16)
    wo_a = w_out_a[0].astype(BF16)
    w_kvb = w_kv.astype(BF16)
    w_b = w_in_b[0].astype(BF16)
    wo_b = w_out_b[0].astype(BF16)
    knb = jnp.tile(kn_b, HEADS_PER_VREG)[None, :]
    qnb = jnp.tile(qn_b[0], HEADS_PER_VREG)[None, :]
    post_consts = (wo_a, g_kv[None, :], w_kvb, knb, g_b[0][None, :], w_b, qnb)

    xp = x_prompt[0]
    qa, ka, va, k_p, v_p, gate_p = _proj_a_prompt(xp, w_a, g_a[0][None, :], gain_a, cos_p, sin_p, b2,
                                                  n_heads, A_KV_HEADS)
    o_p = _moba_prompt(qa, ka, va)
    h_p, kb_p, vb_p, qb_p, gateb_p = _post_a(o_p, gate_p, xp, *post_consts, cos_p, sin_p, b2, MOBA_BLOCK)
    y_p = _swa_prompt(qb_p, kb_p, vb_p, sinks_b[0], gateb_p, h_p, wo_b)

    xs = x_sample[:, 0]
    q_s, k_s, v_s, gate_s = _proj_a_sample(xs, w_a, g_a[0][None, :], gain_a, cos_s, sin_s, b2,
                                           n_heads, A_KV_HEADS)
    feature_major = lambda a: jnp.moveaxis(a, -3, -1).reshape(a.shape[:-3] + (a.shape[-2] * HEAD_DIM, a.shape[-3]))
    o_s = _moba_sample(page_table, _slot_rows(q_s, n_heads, A_KV_HEADS), k_s[:, None, :], v_s[:, None, :],
                       feature_major(cache_a_k[0]), feature_major(cache_a_v[0]))
    h_s, kb_s, vb_s, qb_s, gateb_s = _post_a(o_s.reshape(n, n_heads * HEAD_DIM), gate_s, xs, *post_consts,
                                             cos_s, sin_s, b2, n)
    ob_s, win_skt, win_svt = _swa_sample(_slot_rows(qb_s, b_heads, B_KV_HEADS),
                                         feature_major(state_b_k), feature_major(state_b_v),
                                         kb_s, vb_s, sinks_b[0][:, None])
    y_s = _gated_out(ob_s.reshape(n, b_heads * HEAD_DIM), gateb_s, h_s, wo_b)
    token_major = lambda a: jnp.moveaxis(a.reshape(n, B_KV_HEADS, HEAD_DIM, WINDOW), -1, 1)
    win_sk, win_sv = token_major(win_skt), token_major(win_svt)

    return (y_p[None], y_s[:, None, :],
            k_p.reshape(1, 1, s, A_KV_HEADS, HEAD_DIM), v_p.reshape(1, 1, s, A_KV_HEADS, HEAD_DIM),
            kb_p[s - WINDOW:].reshape(1, WINDOW, B_KV_HEADS, HEAD_DIM),
            vb_p[s - WINDOW:].reshape(1, WINDOW, B_KV_HEADS, HEAD_DIM),
            k_s.reshape(1, n, 1, A_KV_HEADS, HEAD_DIM), v_s.reshape(1, n, 1, A_KV_HEADS, HEAD_DIM),
            win_sk, win_sv)
```

```python
import functools

import jax
import jax.numpy as jnp
from jax import lax
from jax.experimental import pallas as pl
from jax.experimental.pallas import tpu as pltpu

F32 = jnp.float32
BF16 = jnp.bfloat16

HEAD_DIM = 64
HALF = HEAD_DIM // 2
A_KV_HEADS = 4
B_KV_HEADS = 2
MOBA_BLOCK = 256
MOBA_TOPK = 3
PAGE_SIZE = 128
WINDOW = 128
ROPE_THETA = 10000.0
EPS = 1e-6
NEG_INF = -1e30
SCALE = HEAD_DIM ** -0.5

LANES = 128
HEADS_PER_VREG = LANES // HEAD_DIM
VMEM_LIMIT = 56 * 1024 * 1024

LOG2E = 1.4426950408889634
MOBA_KEY_STEP = 4 * MOBA_BLOCK
MOBA_ROW_HEADS = 2
SWA_SUB = 2

PAGES_PER_CHUNK = 32
RING = 4
LOOKAHEAD = RING - 1


def _nt(a, b):
    return lax.dot_general(a, b, (((1,), (1,)), ((), ())), preferred_element_type=F32)


def _split(x):
    hi = x.astype(BF16)
    lo = (x - hi.astype(F32)).astype(BF16)
    return hi, lo


def _nt3(a_split, b_split):
    a_hi, a_lo = a_split
    b_hi, b_lo = b_split
    return _nt(a_hi, b_hi) + _nt(a_hi, b_lo) + _nt(a_lo, b_hi)


def _silu(g):
    return g * (1.0 / (1.0 + jnp.exp(-g)))


def _rms_scale(x):
    return lax.rsqrt(jnp.mean(x * x, axis=-1, keepdims=True) + EPS)


def _head_norm_rope(zz, gain, cos, sin, b2, first_half):
    ss = jnp.dot((zz * zz).astype(BF16), b2, preferred_element_type=F32)
    zh = zz * lax.rsqrt(ss * (1.0 / HEAD_DIM) + EPS) * gain
    partner = jnp.where(first_half, pltpu.roll(zh, LANES - HALF, 1), pltpu.roll(zh, HALF, 1))
    return zh * cos + partner * sin


def _top3_keep(g, valid, lane):
    cand = jnp.where(valid, g, NEG_INF)
    sel = jnp.zeros(g.shape, jnp.bool_)
    lane_f = lane.astype(F32)
    for _ in range(MOBA_TOPK):
        mx = jnp.max(cand, axis=1, keepdims=True)
        first = jnp.min(jnp.where(cand == mx, lane_f, 2.0 * LANES), axis=1, keepdims=True)
        pick = lane_f == first
        sel = sel | pick
        cand = jnp.where(pick, -jnp.inf, cand)
    return sel & valid


def _top3_keep_cols(g, valid, row):
    cand = jnp.where(valid, g, NEG_INF)
    sel = jnp.zeros(g.shape, jnp.bool_)
    row_f = row.astype(F32)
    for _ in range(MOBA_TOPK):
        mx = jnp.max(cand, axis=0, keepdims=True)
        first = jnp.min(jnp.where(cand == mx, row_f, 2.0 * LANES), axis=0, keepdims=True)
        pick = row_f == first
        sel = sel | pick
        cand = jnp.where(pick, -jnp.inf, cand)
    return sel & valid


def _proj_a_prompt_kernel(x_ref, w_ref, g_ref, gain_ref, cos_ref, sin_ref, b2_ref,
                          qa_ref, ka_ref, va_ref, k_ref, v_ref, gate_ref, kmt_ref,
                          *, n_heads, n_kv):
    i = pl.program_id(0)
    tm = x_ref.shape[0]
    a_width = n_heads * HEAD_DIM
    kv_width = n_kv * HEAD_DIM

    @pl.when(i == 0)
    def _():
        kmt_ref[...] = jnp.zeros(kmt_ref.shape, F32)

    x = x_ref[...]
    xb = (x * _rms_scale(x) * g_ref[...]).astype(BF16)
    z = jnp.dot(xb, w_ref[...], preferred_element_type=F32)

    lane = lax.broadcasted_iota(jnp.int32, (tm, LANES), 1)
    first_half = (lane % HEAD_DIM) < HALF
    low = lane < HEAD_DIM
    cos = cos_ref[...]
    sin = sin_ref[...]
    b2 = b2_ref[...]
    own = (lane - HEAD_DIM) == i
    blk = lax.broadcasted_iota(jnp.int32, (HEAD_DIM, tm), 0)
    pad_rows = jnp.full((LANES - HEAD_DIM, tm), NEG_INF, F32)

    group = n_heads // n_kv
    kmt_split = [[_split(kmt_ref[kvh, e]) for e in range(HEADS_PER_VREG)] for kvh in range(n_kv)]
    for c in range(a_width // LANES):
        ro = _head_norm_rope(z[:, c * LANES:(c + 1) * LANES], gain_ref[:, c * LANES:(c + 1) * LANES],
                             cos, sin, b2, first_half)
        ro_split = _split(ro)
        qs = ro * (SCALE * LOG2E)
        qs_r = pltpu.roll(qs, HEAD_DIM, 1)
        for e in range(HEADS_PER_VREG):
            gt = _nt3(kmt_split[(c * HEADS_PER_VREG) // group][e], ro_split)[HEAD_DIM:]
            keep = _top3_keep_cols(gt, blk < i, blk) | (blk == i)
            bias_t = jnp.concatenate([pad_rows, jnp.where(keep, 0.0, NEG_INF)], axis=0)
            qa = jnp.where(low, qs if e == 0 else qs_r, bias_t.T)
            qa_ref[c * HEADS_PER_VREG + e] = qa.astype(BF16)

    onehot = jnp.where(own, 1.0, 0.0)
    for c in range(kv_width // LANES):
        col = a_width + c * LANES
        ro = _head_norm_rope(z[:, col:col + LANES], gain_ref[:, col:col + LANES], cos, sin, b2, first_half)
        k_ref[:, c * LANES:(c + 1) * LANES] = ro
        ro_r = pltpu.roll(ro, HEAD_DIM, 1)
        cs = jnp.sum(ro, axis=0, keepdims=True) * (1.0 / MOBA_BLOCK)
        cs_r = jnp.sum(ro_r, axis=0, keepdims=True) * (1.0 / MOBA_BLOCK)
        low1 = low[0:1]
        for e in range(HEADS_PER_VREG):
            kvh = c * HEADS_PER_VREG + e
            ka_ref[kvh] = jnp.where(low, ro if e == 0 else ro_r, onehot).astype(BF16)
            kmt_ref[kvh, 0, pl.ds(HEAD_DIM + i, 1), :] = jnp.where(low1, cs if e == 0 else cs_r, 0.0)
            kmt_ref[kvh, 1, pl.ds(HEAD_DIM + i, 1), :] = jnp.where(low1, 0.0, cs_r if e == 0 else cs)

    ones_col = jnp.where(lane == HEAD_DIM, 1.0, 0.0)
    v0 = a_width + kv_width
    v_ref[...] = z[:, v0:v0 + kv_width]
    for c in range(kv_width // LANES):
        vv = z[:, v0 + c * LANES:v0 + (c + 1) * LANES]
        vv_r = pltpu.roll(vv, HEAD_DIM, 1)
        for e in range(HEADS_PER_VREG):
            va_ref[c * HEADS_PER_VREG + e] = jnp.where(low, vv if e == 0 else vv_r, ones_col).astype(BF16)

    gate_ref[...] = z[:, v0 + kv_width:]


def _proj_a_sample_kernel(x_ref, w_ref, g_ref, gain_ref, cos_ref, sin_ref, b2_ref,
                          q_ref, k_ref, v_ref, gate_ref, *, n_heads, n_kv):
    tm = x_ref.shape[0]
    a_width = n_heads * HEAD_DIM
    kv_width = n_kv * HEAD_DIM
    x = x_ref[...]
    xb = (x * _rms_scale(x) * g_ref[...]).astype(BF16)
    z = jnp.dot(xb, w_ref[...], preferred_element_type=F32)
    lane = lax.broadcasted_iota(jnp.int32, (tm, LANES), 1)
    first_half = (lane % HEAD_DIM) < HALF
    cos = cos_ref[...]
    sin = sin_ref[...]
    b2 = b2_ref[...]
    for c in range((a_width + kv_width) // LANES):
        ro = _head_norm_rope(z[:, c * LANES:(c + 1) * LANES], gain_ref[:, c * LANES:(c + 1) * LANES],
                             cos, sin, b2, first_half)
        if c * LANES < a_width:
            q_ref[:, c * LANES:(c + 1) * LANES] = ro
        else:
            k_ref[:, c * LANES - a_width:(c + 1) * LANES - a_width] = ro
    v0 = a_width + kv_width
    v_ref[...] = z[:, v0:v0 + kv_width]
    gate_ref[...] = z[:, v0 + kv_width:]


def _const_spec(shape):
    return pl.BlockSpec(shape, lambda *_: (0,) * len(shape))


def _proj_a_prompt(x, w, g, gain, cos, sin, b2, n_heads, n_kv):
    s, d = x.shape
    tm = MOBA_BLOCK
    n_in = w.shape[1]
    a_width = n_heads * HEAD_DIM
    kv_width = n_kv * HEAD_DIM
    row = lambda width: pl.BlockSpec((tm, width), lambda i: (i, 0))
    head_plane = lambda n: pl.BlockSpec((n, tm, LANES), lambda i: (0, i, 0))
    return pl.pallas_call(
        functools.partial(_proj_a_prompt_kernel, n_heads=n_heads, n_kv=n_kv),
        grid=(s // tm,),
        in_specs=[row(d), _const_spec((d, n_in)), _const_spec((1, d)), _const_spec((1, a_width + kv_width)),
                  row(LANES), row(LANES), _const_spec((LANES, LANES))],
        out_specs=[head_plane(n_heads), head_plane(n_kv), head_plane(n_kv),
                   row(kv_width), row(kv_width), row(a_width)],
        out_shape=[jax.ShapeDtypeStruct((n_heads, s, LANES), BF16),
                   jax.ShapeDtypeStruct((n_kv, s, LANES), BF16),
                   jax.ShapeDtypeStruct((n_kv, s, LANES), BF16),
                   jax.ShapeDtypeStruct((s, kv_width), F32),
                   jax.ShapeDtypeStruct((s, kv_width), F32),
                   jax.ShapeDtypeStruct((s, a_width), F32)],
        scratch_shapes=[pltpu.VMEM((n_kv, HEADS_PER_VREG, LANES, LANES), F32)],
        compiler_params=pltpu.CompilerParams(dimension_semantics=("arbitrary",),
                                             vmem_limit_bytes=VMEM_LIMIT),
        name="proj_a_prompt",
    )(x, w, g, gain, cos, sin, b2)


def _proj_a_sample(x, w, g, gain, cos, sin, b2, n_heads, n_kv):
    n, d = x.shape
    a_width = n_heads * HEAD_DIM
    kv_width = n_kv * HEAD_DIM
    full = lambda a: _const_spec(a.shape)
    out_shape = [jax.ShapeDtypeStruct((n, a_width), F32), jax.ShapeDtypeStruct((n, kv_width), F32),
                 jax.ShapeDtypeStruct((n, kv_width), F32), jax.ShapeDtypeStruct((n, a_width), F32)]
    return pl.pallas_call(
        functools.partial(_proj_a_sample_kernel, n_heads=n_heads, n_kv=n_kv),
        grid=(1,),
        in_specs=[full(a) for a in (x, w, g, gain, cos, sin, b2)],
        out_specs=[_const_spec(o.shape) for o in out_shape],
        out_shape=out_shape,
        compiler_params=pltpu.CompilerParams(vmem_limit_bytes=VMEM_LIMIT),
        name="proj_a_sample",
    )(x, w, g, gain, cos, sin, b2)


def _moba_prompt_kernel(q_ref, k_ref, v_ref, o_ref, s_buf, last_buf, m_ref, acc_ref):
    i = pl.program_id(1)
    group, tq, _ = q_ref.shape
    rows = MOBA_ROW_HEADS * tq
    span = MOBA_KEY_STEP
    n_full = (i * MOBA_BLOCK) // span
    last0 = pl.multiple_of(n_full * span, span)
    lane_groups = span // LANES

    def fold_max(s):
        parts = [s[:, c * LANES:(c + 1) * LANES] for c in range(lane_groups)]
        while len(parts) > 1:
            parts = [jnp.maximum(a, b) for a, b in zip(parts[::2], parts[1::2])]
        return parts[0]

    def q_rows(pair):
        return q_ref[pair * MOBA_ROW_HEADS:(pair + 1) * MOBA_ROW_HEADS].reshape(rows, LANES)

    def probs(s, pair):
        mb = m_ref[pair % 2]
        return jnp.concatenate([jnp.exp2(s[:, c * LANES:(c + 1) * LANES] - mb) for c in range(lane_groups)],
                               axis=1).astype(BF16)

    tok = lax.broadcasted_iota(jnp.int32, (rows, span), 0) % tq + i * tq
    key = lax.broadcasted_iota(jnp.int32, (rows, span), 1) + n_full * span
    causal = key <= tok

    def score_last(pair):
        s_last = jnp.where(causal, _nt(q_rows(pair), k_ref[pl.ds(last0, span), :]), NEG_INF)
        last_buf[pair % 2] = s_last
        m_ref[pair % 2] = fold_max(s_last)

    def score_step(pair, t):
        k0 = pl.multiple_of(t * span, span)
        s = _nt(q_rows(pair), k_ref[pl.ds(k0, span), :])
        s_buf[:, pl.ds(k0, span)] = s
        m_ref[pair % 2] = jnp.maximum(m_ref[pair % 2], fold_max(s))

    def pv_last(pair):
        m_ref[pair % 2] = jnp.broadcast_to(jnp.max(m_ref[pair % 2], axis=1, keepdims=True), (rows, LANES))
        acc_ref[...] = jnp.dot(probs(last_buf[pair % 2], pair), v_ref[pl.ds(last0, span), :],
                               preferred_element_type=F32)

    def pv_step(pair, t):
        k0 = pl.multiple_of(t * span, span)
        acc_ref[...] += jnp.dot(probs(s_buf[:, pl.ds(k0, span)], pair), v_ref[pl.ds(k0, span), :],
                                preferred_element_type=F32)

    n_pairs = group // MOBA_ROW_HEADS
    score_last(0)

    def first_body(t, carry):
        score_step(0, t)
        return carry

    lax.fori_loop(0, n_full, first_body, 0)
    for pair in range(n_pairs):
        pv_last(pair)
        if pair + 1 < n_pairs:
            score_last(pair + 1)

        def body(t, carry, pair=pair):
            pv_step(pair, t)
            if pair + 1 < n_pairs:
                score_step(pair + 1, t)
            return carry

        lax.fori_loop(0, n_full, body, 0)
        acc = acc_ref[...]
        o = acc[:, :HEAD_DIM] / acc[:, HEAD_DIM:HEAD_DIM + 1]
        for g in range(MOBA_ROW_HEADS):
            hd = pair * MOBA_ROW_HEADS + g
            o_ref[:, hd * HEAD_DIM:(hd + 1) * HEAD_DIM] = o[g * tq:(g + 1) * tq]


def _moba_prompt(qa, ka, va):
    n_heads, s, _ = qa.shape
    n_kv = ka.shape[0]
    group = n_heads // n_kv
    tq = MOBA_BLOCK
    assert group % MOBA_ROW_HEADS == 0 and s % MOBA_KEY_STEP == 0
    rows = MOBA_ROW_HEADS * tq
    whole_seq = pl.BlockSpec((None, s, LANES), lambda c, i: (c, 0, 0), pipeline_mode=pl.Buffered(1))
    return pl.pallas_call(
        _moba_prompt_kernel,
        grid=(n_kv, s // tq),
        in_specs=[pl.BlockSpec((group, tq, LANES), lambda c, i: (c, i, 0)), whole_seq, whole_seq],
        out_specs=pl.BlockSpec((tq, group * HEAD_DIM), lambda c, i: (i, c)),
        out_shape=jax.ShapeDtypeStruct((s, n_heads * HEAD_DIM), F32),
        scratch_shapes=[pltpu.VMEM((rows, s), F32), pltpu.VMEM((2, rows, MOBA_KEY_STEP), F32),
                        pltpu.VMEM((2, rows, LANES), F32), pltpu.VMEM((rows, LANES), F32)],
        compiler_params=pltpu.CompilerParams(dimension_semantics=("arbitrary", "arbitrary"),
                                             vmem_limit_bytes=VMEM_LIMIT),
        name="moba_prompt",
    )(qa, ka, va)


def _post_a_kernel(o_ref, gate_ref, x_ref, wo_ref, gkv_ref, wkv_ref, knb_ref, gb_ref, wb_ref, qnb_ref,
                   cos_ref, sin_ref, b2_ref, h_ref, kb_ref, vb_ref, qb_ref, gateb_ref):
    tm = x_ref.shape[0]
    b_width = qb_ref.shape[1]
    h = x_ref[...] + jnp.dot((o_ref[...] * _silu(gate_ref[...])).astype(BF16), wo_ref[...],
                             preferred_element_type=F32)
    h_ref[...] = h
    hn = h * _rms_scale(h)
    lane = lax.broadcasted_iota(jnp.int32, (tm, LANES), 1)
    first_half = (lane % HEAD_DIM) < HALF
    cos = cos_ref[...]
    sin = sin_ref[...]
    b2 = b2_ref[...]

    zkv = jnp.dot((hn * gkv_ref[...]).astype(BF16), wkv_ref[...], preferred_element_type=F32)
    kvw = kb_ref.shape[1]
    for c in range(kvw // LANES):
        kb_ref[:, c * LANES:(c + 1) * LANES] = _head_norm_rope(
            zkv[:, c * LANES:(c + 1) * LANES], knb_ref[...], cos, sin, b2, first_half)
    vb_ref[...] = zkv[:, kvw:]

    zb = jnp.dot((hn * gb_ref[...]).astype(BF16), wb_ref[...], preferred_element_type=F32)
    for c in range(b_width // LANES):
        ro = _head_norm_rope(zb[:, c * LANES:(c + 1) * LANES], qnb_ref[...], cos, sin, b2, first_half)
        qb_ref[:, c * LANES:(c + 1) * LANES] = (ro * SCALE).astype(BF16)
    gateb_ref[...] = zb[:, b_width:]


def _post_a(o, gate, x, wo, gkv, wkv, knb, gb, wb, qnb, cos, sin, b2, tm):
    s, d = x.shape
    kvw = wkv.shape[1] // 2
    b_width = wb.shape[1] // 2
    row = lambda width: pl.BlockSpec((tm, width), lambda i: (i, 0))
    consts = (wo, gkv, wkv, knb, gb, wb, qnb)
    return pl.pallas_call(
        _post_a_kernel,
        grid=(s // tm,),
        in_specs=[row(o.shape[1]), row(gate.shape[1]), row(d)] + [_const_spec(a.shape) for a in consts]
                 + [row(LANES), row(LANES), _const_spec(b2.shape)],
        out_specs=[row(d), row(kvw), row(kvw), row(b_width), row(b_width)],
        out_shape=[jax.ShapeDtypeStruct((s, d), F32), jax.ShapeDtypeStruct((s, kvw), F32),
                   jax.ShapeDtypeStruct((s, kvw), F32), jax.ShapeDtypeStruct((s, b_width), BF16),
                   jax.ShapeDtypeStruct((s, b_width), F32)],
        compiler_params=pltpu.CompilerParams(dimension_semantics=("parallel",),
                                             vmem_limit_bytes=VMEM_LIMIT),
        name="post_a",
    )(o, gate, x, *consts, cos, sin, b2)


def _swa_prompt_kernel(q_ref, kc_ref, kp_ref, vc_ref, vp_ref, sink_ref, gate_ref, h_ref, wo_ref,
                       y_ref, o_scr):
    n = pl.program_id(0)
    w = kp_ref.shape[0]
    sub = q_ref.shape[0] // w
    n_heads = q_ref.shape[1] // HEAD_DIM
    n_kv = kc_ref.shape[1] // HEAD_DIM
    group = n_heads // n_kv
    assert n_kv == HEADS_PER_VREG
    k_all = jnp.concatenate([kp_ref[...], kc_ref[...]], axis=0).astype(BF16)
    v_all = jnp.concatenate([vp_ref[...], vc_ref[...]], axis=0).astype(BF16)
    rows = group * w
    t = lax.broadcasted_iota(jnp.int32, (rows, 2 * w), 0) % w
    j = lax.broadcasted_iota(jnp.int32, (rows, 2 * w), 1)
    band = (j >= t) & (j <= t + w)
    lane = lax.broadcasted_iota(jnp.int32, (w, LANES), 1)

    for u in range(sub):
        kk = k_all[u * w:(u + 2) * w]
        vv = v_all[u * w:(u + 2) * w]
        mask = band & ((n > 0) | (j >= w)) if u == 0 else band
        for c in range(n_kv):
            in_slot = (lane // HEAD_DIM) == c
            parts = []
            sinks = []
            for g in range(group):
                hd = c * group + g
                col = (hd // HEADS_PER_VREG) * LANES
                vreg = q_ref[u * w:(u + 1) * w, col:col + LANES].astype(F32)
                if hd % HEADS_PER_VREG != c:
                    vreg = pltpu.roll(vreg, HEAD_DIM, 1)
                parts.append(jnp.where(in_slot, vreg, 0.0).astype(BF16))
                sinks.append(jnp.full((w, 1), sink_ref[hd], F32))
            qs = jnp.concatenate(parts, axis=0)
            sk = jnp.concatenate(sinks, axis=0)
            s = jnp.where(mask, _nt(qs, kk), NEG_INF)
            m = jnp.maximum(jnp.max(s, axis=1, keepdims=True), sk)
            e = jnp.exp(s - m)
            inv = 1.0 / (jnp.sum(e, axis=1, keepdims=True) + jnp.exp(sk - m))
            o = jnp.dot(e.astype(BF16), vv, preferred_element_type=F32) * inv
            for g in range(0, group, HEADS_PER_VREG):
                hd = c * group + g
                even = o[g * w:(g + 1) * w]
                odd = o[(g + 1) * w:(g + 2) * w]
                if c == 0:
                    odd = pltpu.roll(odd, HEAD_DIM, 1)
                else:
                    even = pltpu.roll(even, HEAD_DIM, 1)
                dst = (hd // HEADS_PER_VREG) * LANES
                o_scr[u * w:(u + 1) * w, dst:dst + LANES] = jnp.where(lane < HEAD_DIM, even, odd)

    y_ref[...] = h_ref[...] + jnp.dot((o_scr[...] * _silu(gate_ref[...])).astype(BF16), wo_ref[...],
                                      preferred_element_type=F32)


def _swa_prompt(qb, kb, vb, sinks, gateb, h, wo):
    s, b_width = qb.shape
    kvw = kb.shape[1]
    d = h.shape[1]
    w = WINDOW
    tb = SWA_SUB * w
    cur = lambda width: pl.BlockSpec((tb, width), lambda n: (n, 0))
    prev = lambda width: pl.BlockSpec((w, width), lambda n: (jnp.maximum(n * SWA_SUB - 1, 0), 0))
    return pl.pallas_call(
        _swa_prompt_kernel,
        grid=(s // tb,),
        in_specs=[cur(b_width), cur(kvw), prev(kvw), cur(kvw), prev(kvw),
                  pl.BlockSpec(memory_space=pltpu.SMEM), cur(b_width), cur(d), _const_spec(wo.shape)],
        out_specs=cur(d),
        out_shape=jax.ShapeDtypeStruct((s, d), F32),
        scratch_shapes=[pltpu.VMEM((tb, b_width), F32)],
        compiler_params=pltpu.CompilerParams(dimension_semantics=("parallel",),
                                             vmem_limit_bytes=VMEM_LIMIT),
        name="swa_prompt",
    )(qb, kb, kb, vb, vb, sinks, gateb, h, wo)


def _moba_sample_kernel(pt_ref, wq_ref, knew_ref, vnew_ref, expand_ref, kc_ref, vc_ref,
                        o_ref, buf, sem, s_scr, p_scr, ksum_scr):
    b = pl.program_id(0)
    n_samples = pl.num_programs(0)
    n_heads = wq_ref.shape[1]
    kvw = wq_ref.shape[2]
    n_blocks = s_scr.shape[1] // MOBA_BLOCK
    blocks_per_chunk = PAGES_PER_CHUNK * PAGE_SIZE // MOBA_BLOCK
    assert MOBA_BLOCK == 2 * PAGE_SIZE and n_blocks <= LANES
    jobs = 2 * RING

    def page_copy(sample, job, p):
        cache = kc_ref if job < RING else vc_ref
        slot = job % RING
        page = pt_ref[sample, (job % RING) * PAGES_PER_CHUNK + p]
        return pltpu.make_async_copy(cache.at[page], buf.at[slot, p], sem.at[slot])

    def issue(sample, job):
        def body(p, carry):
            page_copy(sample, job, p).start()
            return carry
        lax.fori_loop(0, PAGES_PER_CHUNK, body, 0)

    def wait(sample, job):
        def body(p, carry):
            page_copy(sample, job, p).wait()
            return carry
        lax.fori_loop(0, PAGES_PER_CHUNK, body, 0)

    @pl.when(b == 0)
    def _():
        ksum_scr[...] = jnp.zeros(ksum_scr.shape, F32)
        for job in range(LOOKAHEAD):
            issue(b, job)

    def prefetch(job):
        nxt = job + LOOKAHEAD
        if nxt < jobs:
            issue(b, nxt)
        else:
            @pl.when(b + 1 < n_samples)
            def _():
                issue(b + 1, nxt - jobs)

    wq = wq_ref[0]
    wqs = (wq * SCALE).astype(BF16)

    lane_k = lax.broadcasted_iota(jnp.int32, (kvw, LANES), 1)
    for job in range(RING):
        wait(b, job)
        prefetch(job)

        def k_body(r, carry, job=job):
            kt0 = buf[job, 2 * r]
            kt1 = buf[job, 2 * r + 1]
            blk = job * blocks_per_chunk + r
            kt = jnp.concatenate([kt0, kt1], axis=1).astype(BF16)
            off = pl.multiple_of(blk * MOBA_BLOCK, MOBA_BLOCK)
            s_scr[:, pl.ds(off, MOBA_BLOCK)] = jnp.dot(wqs, kt, preferred_element_type=F32)
            col = jnp.sum(kt0 + kt1, axis=1, keepdims=True)
            ksum_scr[...] = jnp.where(lane_k == blk, col, ksum_scr[...])
            return carry

        lax.fori_loop(0, blocks_per_chunk, k_body, 0, unroll=4)

    kmean = ksum_scr[...] * (1.0 / MOBA_BLOCK)
    wq_hi, wq_lo = _split(wq)
    km_hi, km_lo = _split(kmean)
    dot = functools.partial(jnp.dot, preferred_element_type=F32)
    gate = dot(wq_hi, km_hi) + dot(wq_hi, km_lo) + dot(wq_lo, km_hi)
    lane = lax.broadcasted_iota(jnp.int32, gate.shape, 1)
    keep = _top3_keep(gate, lane < n_blocks, lane)
    bias = jnp.where(keep, 0.0, NEG_INF).astype(BF16)
    s = s_scr[...] + dot(bias, expand_ref[...])
    s_self = jnp.sum(wq * knew_ref[0], axis=1, keepdims=True) * SCALE
    m = jnp.maximum(jnp.max(s, axis=1, keepdims=True), s_self)
    p = jnp.exp(s - m)
    p_self = jnp.exp(s_self - m)
    denom = jnp.sum(p, axis=1, keepdims=True) + p_self
    p_scr[...] = p.astype(BF16)

    acc = jnp.zeros((n_heads, kvw), F32)
    for job in range(RING, jobs):
        wait(b, job)
        prefetch(job)

        def v_body(r, acc, job=job):
            vt = jnp.concatenate([buf[job % RING, 2 * r], buf[job % RING, 2 * r + 1]], axis=1)
            off = pl.multiple_of(((job - RING) * blocks_per_chunk + r) * MOBA_BLOCK, MOBA_BLOCK)
            return acc + _nt(p_scr[:, pl.ds(off, MOBA_BLOCK)], vt.astype(BF16))

        acc = lax.fori_loop(0, blocks_per_chunk, v_body, acc, unroll=4)

    o = (acc + p_self * vnew_ref[0]) / denom
    head = lax.broadcasted_iota(jnp.int32, (n_heads, HEAD_DIM), 0)
    group = n_heads // (kvw // HEAD_DIM)
    out = jnp.zeros((n_heads, HEAD_DIM), F32)
    for c in range(kvw // HEAD_DIM):
        out = jnp.where(head // group == c, o[:, c * HEAD_DIM:(c + 1) * HEAD_DIM], out)
    o_ref[0] = out


def _moba_sample(page_table, wq, knew, vnew, cache_kt, cache_vt):
    n, n_heads, kvw = wq.shape
    n_pages = page_table.shape[1]
    assert n_pages == RING * PAGES_PER_CHUNK and cache_kt.shape[1:] == (kvw, PAGE_SIZE)
    n_keys = n_pages * PAGE_SIZE
    expand = (jnp.arange(LANES)[:, None] == jnp.arange(n_keys)[None, :] // MOBA_BLOCK).astype(BF16)
    grid_spec = pltpu.PrefetchScalarGridSpec(
        num_scalar_prefetch=1,
        grid=(n,),
        in_specs=[pl.BlockSpec((1, n_heads, kvw), lambda b, pt: (b, 0, 0)),
                  pl.BlockSpec((1, 1, kvw), lambda b, pt: (b, 0, 0)),
                  pl.BlockSpec((1, 1, kvw), lambda b, pt: (b, 0, 0)),
                  pl.BlockSpec(expand.shape, lambda b, pt: (0, 0)),
                  pl.BlockSpec(memory_space=pl.ANY),
                  pl.BlockSpec(memory_space=pl.ANY)],
        out_specs=pl.BlockSpec((1, n_heads, HEAD_DIM), lambda b, pt: (b, 0, 0)),
        scratch_shapes=[pltpu.VMEM((RING, PAGES_PER_CHUNK, kvw, PAGE_SIZE), F32),
                        pltpu.SemaphoreType.DMA((RING,)),
                        pltpu.VMEM((n_heads, n_keys), F32),
                        pltpu.VMEM((n_heads, n_keys), BF16),
                        pltpu.VMEM((kvw, LANES), F32)],
    )
    return pl.pallas_call(
        _moba_sample_kernel,
        grid_spec=grid_spec,
        out_shape=jax.ShapeDtypeStruct((n, n_heads, HEAD_DIM), F32),
        compiler_params=pltpu.CompilerParams(dimension_semantics=("arbitrary",),
                                             vmem_limit_bytes=VMEM_LIMIT),
        name="moba_sample",
    )(page_table, wq, knew, vnew, expand, cache_kt, cache_vt)


def _swa_sample_kernel(wq_ref, sk_ref, sv_ref, knew_ref, vnew_ref, knewt_ref, vnewt_ref, sink_ref,
                       o_ref, wk_ref, wv_ref):
    nb, n_heads, kvw = wq_ref.shape
    w = sk_ref.shape[2]
    group = n_heads // (kvw // HEAD_DIM)
    head = lax.broadcasted_iota(jnp.int32, (n_heads, HEAD_DIM), 0)
    pos = lax.broadcasted_iota(jnp.int32, (kvw, w), 1)
    col_id = lax.broadcasted_iota(jnp.int32, knewt_ref.shape, 1)
    sink = sink_ref[...]
    for i in range(nb):
        wq = wq_ref[i]
        kt = sk_ref[i]
        vt = sv_ref[i]
        knew = knew_ref[i]
        vnew = vnew_ref[i]
        s = jnp.dot(wq, kt.astype(BF16), preferred_element_type=F32)
        s_new = jnp.sum(wq.astype(F32) * knew, axis=1, keepdims=True)
        m = jnp.maximum(jnp.maximum(jnp.max(s, axis=1, keepdims=True), s_new), sink)
        e = jnp.exp(s - m)
        e_new = jnp.exp(s_new - m)
        denom = jnp.sum(e, axis=1, keepdims=True) + e_new + jnp.exp(sink - m)
        o = _nt((e / denom).astype(BF16), vt.astype(BF16)) + (e_new / denom) * vnew
        out = jnp.zeros((n_heads, HEAD_DIM), F32)
        for c in range(kvw // HEAD_DIM):
            out = jnp.where(head // group == c, o[:, c * HEAD_DIM:(c + 1) * HEAD_DIM], out)
        o_ref[i] = out
        mine = col_id == pl.program_id(0) * nb + i
        knew_col = jnp.sum(jnp.where(mine, knewt_ref[...], 0.0), axis=1, keepdims=True)
        vnew_col = jnp.sum(jnp.where(mine, vnewt_ref[...], 0.0), axis=1, keepdims=True)
        wk_ref[i] = jnp.where(pos == w - 1, knew_col, pltpu.roll(kt, w - 1, 1))
        wv_ref[i] = jnp.where(pos == w - 1, vnew_col, pltpu.roll(vt, w - 1, 1))


def _swa_sample(wq, state_kt, state_vt, knew, vnew, sinks):
    n, n_heads, kvw = wq.shape
    w = state_kt.shape[2]
    nb = 8
    blk = lambda shape: pl.BlockSpec((nb,) + shape, lambda i: (i,) + (0,) * len(shape))
    knew_t = knew.T
    vnew_t = vnew.T
    return pl.pallas_call(
        _swa_sample_kernel,
        grid=(n // nb,),
        in_specs=[blk((n_heads, kvw)), blk((kvw, w)), blk((kvw, w)), blk((1, kvw)), blk((1, kvw)),
                  _const_spec(knew_t.shape), _const_spec(vnew_t.shape), _const_spec(sinks.shape)],
        out_specs=[blk((n_heads, HEAD_DIM)), blk((kvw, w)), blk((kvw, w))],
        out_shape=[jax.ShapeDtypeStruct((n, n_heads, HEAD_DIM), F32),
                   jax.ShapeDtypeStruct((n, kvw, w), F32), jax.ShapeDtypeStruct((n, kvw, w), F32)],
        compiler_params=pltpu.CompilerParams(dimension_semantics=("parallel",),
                                             vmem_limit_bytes=VMEM_LIMIT),
        name="swa_sample",
    )(wq, state_kt, state_vt, knew[:, None, :], vnew[:, None, :], knew_t, vnew_t, sinks)


def _gated_out_kernel(o_ref, gate_ref, h_ref, wo_ref, y_ref):
    y_ref[...] = h_ref[...] + jnp.dot((o_ref[...] * _silu(gate_ref[...])).astype(BF16), wo_ref[...],
                                      preferred_element_type=F32)


def _gated_out(o, gate, h, wo):
    return pl.pallas_call(
        _gated_out_kernel,
        grid=(1,),
        in_specs=[_const_spec(a.shape) for a in (o, gate, h, wo)],
        out_specs=_const_spec(h.shape),
        out_shape=jax.ShapeDtypeStruct(h.shape, F32),
        compiler_params=pltpu.CompilerParams(vmem_limit_bytes=VMEM_LIMIT),
        name="gated_out",
    )(o, gate, h, wo)


def _rope_tables(pos):
    inv = 1.0 / (ROPE_THETA ** (jnp.arange(HALF, dtype=F32) / HALF))
    ang = pos.astype(F32)[:, None] * inv[None, :]
    c, s = jnp.cos(ang), jnp.sin(ang)
    return jnp.tile(c, (1, LANES // HALF)), jnp.tile(jnp.concatenate([-s, s], axis=1), (1, HEADS_PER_VREG))


def _slot_rows(q, n_heads, n_kv):
    n = q.shape[0]
    group = n_heads // n_kv
    onehot = (jnp.arange(n_heads)[:, None] // group == jnp.arange(n_kv)[None, :]).astype(q.dtype)
    q4 = q.reshape(n, n_heads, 1, HEAD_DIM) * onehot[None, :, :, None]
    return q4.reshape(n, n_heads, n_kv * HEAD_DIM)


def kernel(x_prompt, x_sample, cache_a_k, cache_a_v, state_b_k, state_b_v, page_table, g_a, w_in_a, qn_a, kn_a, w_out_a, g_kv, w_kv, kn_b, g_b, w_in_b, qn_b, sinks_b, w_out_b):
    _, s, d = x_prompt.shape
    n, t, _ = x_sample.shape
    assert t == 1 and g_a.shape[0] == 1 and g_b.shape[0] == 1
    n_heads = w_out_a.shape[1] // HEAD_DIM
    b_heads = w_out_b.shape[1] // HEAD_DIM
    past = page_table.shape[1] * PAGE_SIZE
    n_pool = cache_a_k.shape[1]
    a_kvw = A_KV_HEADS * HEAD_DIM
    b_kvw = B_KV_HEADS * HEAD_DIM

    cos_p, sin_p = _rope_tables(jnp.arange(s, dtype=jnp.int32))
    cos_s, sin_s = _rope_tables(jnp.full((n,), past, jnp.int32))
    lane = jnp.arange(LANES)
    b2 = (lane[:, None] // HEAD_DIM == lane[None, :] // HEAD_DIM).astype(BF16)
    gain_a = jnp.concatenate([jnp.tile(qn_a[0], n_heads), jnp.tile(kn_a[0], A_KV_HEADS)])[None, :]
    w_a = w_in_a[0].astype(BF16)
    wo_a = w_out_a[0].astype(BF16)
    w_kvb = w_kv.astype(BF16)
    w_b = w_in_b[0].astype(BF16)
    wo_b = w_out_b[0].astype(BF16)
    knb = jnp.tile(kn_b, HEADS_PER_VREG)[None, :]
    qnb = jnp.tile(qn_b[0], HEADS_PER_VREG)[None, :]
    post_consts = (wo_a, g_kv[None, :], w_kvb, knb, g_b[0][None, :], w_b, qnb)

    xp = x_prompt[0]
    qa, ka, va, k_p, v_p, gate_p = _proj_a_prompt(xp, w_a, g_a[0][None, :], gain_a, cos_p, sin_p, b2,
                                                  n_heads, A_KV_HEADS)
    o_p = _moba_prompt(qa, ka, va)
    h_p, kb_p, vb_p, qb_p, gateb_p = _post_a(o_p, gate_p, xp, *post_consts, cos_p, sin_p, b2, MOBA_BLOCK)
    y_p = _swa_prompt(qb_p, kb_p, vb_p, sinks_b[0], gateb_p, h_p, wo_b)

    xs = x_sample[:, 0]
    q_s, k_s, v_s, gate_s = _proj_a_sample(xs, w_a, g_a[0][None, :], gain_a, cos_s, sin_s, b2,
                                           n_heads, A_KV_HEADS)
    feature_major = lambda a: jnp.moveaxis(a, -3, -1).reshape(a.shape[:-3] + (a.shape[-2] * HEAD_DIM, a.shape[-3]))
    o_s = _moba_sample(page_table, _slot_rows(q_s, n_heads, A_KV_HEADS), k_s[:, None, :], v_s[:, None, :],
                       feature_major(cache_a_k[0]), feature_major(cache_a_v[0]))
    h_s, kb_s, vb_s, qb_s, gateb_s = _post_a(o_s.reshape(n, n_heads * HEAD_DIM), gate_s, xs, *post_consts,
                                             cos_s, sin_s, b2, n)
    ob_s, win_skt, win_svt = _swa_sample(_slot_rows(qb_s, b_heads, B_KV_HEADS),
                                         feature_major(state_b_k), feature_major(state_b_v),
                                         kb_s, vb_s, sinks_b[0][:, None])
    y_s = _gated_out(ob_s.reshape(n, b_heads * HEAD_DIM), gateb_s, h_s, wo_b)
    token_major = lambda a: jnp.moveaxis(a.reshape(n, B_KV_HEADS, HEAD_DIM, WINDOW), -1, 1)
    win_sk, win_sv = token_major(win_skt), token_major(win_svt)

    return (y_p[None], y_s[:, None, :],
            k_p.reshape(1, 1, s, A_KV_HEADS, HEAD_DIM), v_p.reshape(1, 1, s, A_KV_HEADS, HEAD_DIM),
            kb_p[s - WINDOW:].reshape(1, WINDOW, B_KV_HEADS, HEAD_DIM),
            vb_p[s - WINDOW:].reshape(1, WINDOW, B_KV_HEADS, HEAD_DIM),
            k_s.reshape(1, n, 1, A_KV_HEADS, HEAD_DIM), v_s.reshape(1, n, 1, A_KV_HEADS, HEAD_DIM),
            win_sk, win_sv)
```

```python
import functools

import jax
import jax.numpy as jnp
from jax import lax
from jax.experimental import pallas as pl
from jax.experimental.pallas import tpu as pltpu

F32 = jnp.float32
BF16 = jnp.bfloat16

HEAD_DIM = 64
HALF = HEAD_DIM // 2
A_KV_HEADS = 4
B_KV_HEADS = 2
MOBA_BLOCK = 256
MOBA_TOPK = 3
PAGE_SIZE = 128
WINDOW = 128
ROPE_THETA = 10000.0
EPS = 1e-6
NEG_INF = -1e30
SCALE = HEAD_DIM ** -0.5

LANES = 128
HEADS_PER_VREG = LANES // HEAD_DIM
VMEM_LIMIT = 56 * 1024 * 1024

LOG2E = 1.4426950408889634
MOBA_KEY_STEP = 4 * MOBA_BLOCK
MOBA_ROW_HEADS = 2
MOBA_UNROLL = 2
SWA_SUB = 2

PAGES_PER_CHUNK = 32
RING = 4
LOOKAHEAD = RING - 1


def _nt(a, b):
    return lax.dot_general(a, b, (((1,), (1,)), ((), ())), preferred_element_type=F32)


def _split(x):
    hi = x.astype(BF16)
    lo = (x - hi.astype(F32)).astype(BF16)
    return hi, lo


def _nt3(a_split, b_split):
    a_hi, a_lo = a_split
    b_hi, b_lo = b_split
    return _nt(a_hi, b_hi) + _nt(a_hi, b_lo) + _nt(a_lo, b_hi)


def _silu(g):
    return g * (1.0 / (1.0 + jnp.exp(-g)))


def _rms_scale(x):
    return lax.rsqrt(jnp.mean(x * x, axis=-1, keepdims=True) + EPS)


def _head_norm_rope(zz, gain, cos, sin, b2, first_half):
    ss = jnp.dot((zz * zz).astype(BF16), b2, preferred_element_type=F32)
    zh = zz * lax.rsqrt(ss * (1.0 / HEAD_DIM) + EPS) * gain
    partner = jnp.where(first_half, pltpu.roll(zh, LANES - HALF, 1), pltpu.roll(zh, HALF, 1))
    return zh * cos + partner * sin


def _top3_keep(g, valid, lane):
    cand = jnp.where(valid, g, NEG_INF)
    sel = jnp.zeros(g.shape, jnp.bool_)
    lane_f = lane.astype(F32)
    for _ in range(MOBA_TOPK):
        mx = jnp.max(cand, axis=1, keepdims=True)
        first = jnp.min(jnp.where(cand == mx, lane_f, 2.0 * LANES), axis=1, keepdims=True)
        pick = lane_f == first
        sel = sel | pick
        cand = jnp.where(pick, -jnp.inf, cand)
    return sel & valid


def _top3_keep_cols(g, valid, row):
    cand = jnp.where(valid, g, NEG_INF)
    sel = jnp.zeros(g.shape, jnp.bool_)
    row_f = row.astype(F32)
    for _ in range(MOBA_TOPK):
        mx = jnp.max(cand, axis=0, keepdims=True)
        first = jnp.min(jnp.where(cand == mx, row_f, 2.0 * LANES), axis=0, keepdims=True)
        pick = row_f == first
        sel = sel | pick
        cand = jnp.where(pick, -jnp.inf, cand)
    return sel & valid


def _proj_a_prompt_kernel(x_ref, w_ref, g_ref, gain_ref, cos_ref, sin_ref, b2_ref,
                          qa_ref, ka_ref, va_ref, k_ref, v_ref, gate_ref, kmt_ref,
                          *, n_heads, n_kv):
    i = pl.program_id(0)
    tm = x_ref.shape[0]
    a_width = n_heads * HEAD_DIM
    kv_width = n_kv * HEAD_DIM

    @pl.when(i == 0)
    def _():
        kmt_ref[...] = jnp.zeros(kmt_ref.shape, F32)

    x = x_ref[...]
    xb = (x * _rms_scale(x) * g_ref[...]).astype(BF16)
    z = jnp.dot(xb, w_ref[...], preferred_element_type=F32)

    lane = lax.broadcasted_iota(jnp.int32, (tm, LANES), 1)
    first_half = (lane % HEAD_DIM) < HALF
    low = lane < HEAD_DIM
    cos = cos_ref[...]
    sin = sin_ref[...]
    b2 = b2_ref[...]
    own = (lane - HEAD_DIM) == i
    blk = lax.broadcasted_iota(jnp.int32, (HEAD_DIM, tm), 0)
    pad_rows = jnp.full((LANES - HEAD_DIM, tm), NEG_INF, F32)

    group = n_heads // n_kv
    kmt_split = [[_split(kmt_ref[kvh, e]) for e in range(HEADS_PER_VREG)] for kvh in range(n_kv)]
    for c in range(a_width // LANES):
        ro = _head_norm_rope(z[:, c * LANES:(c + 1) * LANES], gain_ref[:, c * LANES:(c + 1) * LANES],
                             cos, sin, b2, first_half)
        ro_split = _split(ro)
        qs = ro * (SCALE * LOG2E)
        qs_r = pltpu.roll(qs, HEAD_DIM, 1)
        for e in range(HEADS_PER_VREG):
            gt = _nt3(kmt_split[(c * HEADS_PER_VREG) // group][e], ro_split)[HEAD_DIM:]
            keep = _top3_keep_cols(gt, blk < i, blk) | (blk == i)
            bias_t = jnp.concatenate([pad_rows, jnp.where(keep, 0.0, NEG_INF)], axis=0)
            qa = jnp.where(low, qs if e == 0 else qs_r, bias_t.T)
            qa_ref[c * HEADS_PER_VREG + e] = qa.astype(BF16)

    onehot = jnp.where(own, 1.0, 0.0)
    for c in range(kv_width // LANES):
        col = a_width + c * LANES
        ro = _head_norm_rope(z[:, col:col + LANES], gain_ref[:, col:col + LANES], cos, sin, b2, first_half)
        k_ref[:, c * LANES:(c + 1) * LANES] = ro
        ro_r = pltpu.roll(ro, HEAD_DIM, 1)
        cs = jnp.sum(ro, axis=0, keepdims=True) * (1.0 / MOBA_BLOCK)
        cs_r = jnp.sum(ro_r, axis=0, keepdims=True) * (1.0 / MOBA_BLOCK)
        low1 = low[0:1]
        for e in range(HEADS_PER_VREG):
            kvh = c * HEADS_PER_VREG + e
            ka_ref[kvh] = jnp.where(low, ro if e == 0 else ro_r, onehot).astype(BF16)
            kmt_ref[kvh, 0, pl.ds(HEAD_DIM + i, 1), :] = jnp.where(low1, cs if e == 0 else cs_r, 0.0)
            kmt_ref[kvh, 1, pl.ds(HEAD_DIM + i, 1), :] = jnp.where(low1, 0.0, cs_r if e == 0 else cs)

    ones_col = jnp.where(lane == HEAD_DIM, 1.0, 0.0)
    v0 = a_width + kv_width
    v_ref[...] = z[:, v0:v0 + kv_width]
    for c in range(kv_width // LANES):
        vv = z[:, v0 + c * LANES:v0 + (c + 1) * LANES]
        vv_r = pltpu.roll(vv, HEAD_DIM, 1)
        for e in range(HEADS_PER_VREG):
            va_ref[c * HEADS_PER_VREG + e] = jnp.where(low, vv if e == 0 else vv_r, ones_col).astype(BF16)

    gate_ref[...] = z[:, v0 + kv_width:]


def _proj_a_sample_kernel(x_ref, w_ref, g_ref, gain_ref, cos_ref, sin_ref, b2_ref,
                          q_ref, k_ref, v_ref, gate_ref, *, n_heads, n_kv):
    tm = x_ref.shape[0]
    a_width = n_heads * HEAD_DIM
    kv_width = n_kv * HEAD_DIM
    x = x_ref[...]
    xb = (x * _rms_scale(x) * g_ref[...]).astype(BF16)
    z = jnp.dot(xb, w_ref[...], preferred_element_type=F32)
    lane = lax.broadcasted_iota(jnp.int32, (tm, LANES), 1)
    first_half = (lane % HEAD_DIM) < HALF
    cos = cos_ref[...]
    sin = sin_ref[...]
    b2 = b2_ref[...]
    for c in range((a_width + kv_width) // LANES):
        ro = _head_norm_rope(z[:, c * LANES:(c + 1) * LANES], gain_ref[:, c * LANES:(c + 1) * LANES],
                             cos, sin, b2, first_half)
        if c * LANES < a_width:
            q_ref[:, c * LANES:(c + 1) * LANES] = ro
        else:
            k_ref[:, c * LANES - a_width:(c + 1) * LANES - a_width] = ro
    v0 = a_width + kv_width
    v_ref[...] = z[:, v0:v0 + kv_width]
    gate_ref[...] = z[:, v0 + kv_width:]


def _const_spec(shape):
    return pl.BlockSpec(shape, lambda *_: (0,) * len(shape))


def _proj_a_prompt(x, w, g, gain, cos, sin, b2, n_heads, n_kv):
    s, d = x.shape
    tm = MOBA_BLOCK
    n_in = w.shape[1]
    a_width = n_heads * HEAD_DIM
    kv_width = n_kv * HEAD_DIM
    row = lambda width: pl.BlockSpec((tm, width), lambda i: (i, 0))
    head_plane = lambda n: pl.BlockSpec((n, tm, LANES), lambda i: (0, i, 0))
    return pl.pallas_call(
        functools.partial(_proj_a_prompt_kernel, n_heads=n_heads, n_kv=n_kv),
        grid=(s // tm,),
        in_specs=[row(d), _const_spec((d, n_in)), _const_spec((1, d)), _const_spec((1, a_width + kv_width)),
                  row(LANES), row(LANES), _const_spec((LANES, LANES))],
        out_specs=[head_plane(n_heads), head_plane(n_kv), head_plane(n_kv),
                   row(kv_width), row(kv_width), row(a_width)],
        out_shape=[jax.ShapeDtypeStruct((n_heads, s, LANES), BF16),
                   jax.ShapeDtypeStruct((n_kv, s, LANES), BF16),
                   jax.ShapeDtypeStruct((n_kv, s, LANES), BF16),
                   jax.ShapeDtypeStruct((s, kv_width), F32),
                   jax.ShapeDtypeStruct((s, kv_width), F32),
                   jax.ShapeDtypeStruct((s, a_width), F32)],
        scratch_shapes=[pltpu.VMEM((n_kv, HEADS_PER_VREG, LANES, LANES), F32)],
        compiler_params=pltpu.CompilerParams(dimension_semantics=("arbitrary",),
                                             vmem_limit_bytes=VMEM_LIMIT),
        name="proj_a_prompt",
    )(x, w, g, gain, cos, sin, b2)


def _proj_a_sample(x, w, g, gain, cos, sin, b2, n_heads, n_kv):
    n, d = x.shape
    a_width = n_heads * HEAD_DIM
    kv_width = n_kv * HEAD_DIM
    full = lambda a: _const_spec(a.shape)
    out_shape = [jax.ShapeDtypeStruct((n, a_width), F32), jax.ShapeDtypeStruct((n, kv_width), F32),
                 jax.ShapeDtypeStruct((n, kv_width), F32), jax.ShapeDtypeStruct((n, a_width), F32)]
    return pl.pallas_call(
        functools.partial(_proj_a_sample_kernel, n_heads=n_heads, n_kv=n_kv),
        grid=(1,),
        in_specs=[full(a) for a in (x, w, g, gain, cos, sin, b2)],
        out_specs=[_const_spec(o.shape) for o in out_shape],
        out_shape=out_shape,
        compiler_params=pltpu.CompilerParams(vmem_limit_bytes=VMEM_LIMIT),
        name="proj_a_sample",
    )(x, w, g, gain, cos, sin, b2)


def _moba_prompt_kernel(q_ref, k_ref, v_ref, o_ref, s_buf, last_buf, m_ref, acc_ref):
    i = pl.program_id(1)
    group, tq, _ = q_ref.shape
    rows = MOBA_ROW_HEADS * tq
    span = MOBA_KEY_STEP
    n_full = (i * MOBA_BLOCK) // span
    last0 = pl.multiple_of(n_full * span, span)
    lane_groups = span // LANES

    def fold_max(s):
        parts = [s[:, c * LANES:(c + 1) * LANES] for c in range(lane_groups)]
        while len(parts) > 1:
            parts = [jnp.maximum(a, b) for a, b in zip(parts[::2], parts[1::2])]
        return parts[0]

    def q_rows(pair):
        return q_ref[pair * MOBA_ROW_HEADS:(pair + 1) * MOBA_ROW_HEADS].reshape(rows, LANES)

    def probs(s, pair):
        mb = m_ref[pair % 2]
        return jnp.concatenate([jnp.exp2(s[:, c * LANES:(c + 1) * LANES] - mb) for c in range(lane_groups)],
                               axis=1).astype(BF16)

    tok = lax.broadcasted_iota(jnp.int32, (rows, span), 0) % tq + i * tq
    key = lax.broadcasted_iota(jnp.int32, (rows, span), 1) + n_full * span
    causal = key <= tok

    def score_last(pair):
        s_last = jnp.where(causal, _nt(q_rows(pair), k_ref[pl.ds(last0, span), :]), NEG_INF)
        last_buf[pair % 2] = s_last
        m_ref[pair % 2] = fold_max(s_last)

    def score_step(pair, t):
        k0 = pl.multiple_of(t * span, span)
        s = _nt(q_rows(pair), k_ref[pl.ds(k0, span), :])
        s_buf[:, pl.ds(k0, span)] = s
        m_ref[pair % 2] = jnp.maximum(m_ref[pair % 2], fold_max(s))

    def pv_last(pair):
        m_ref[pair % 2] = jnp.broadcast_to(jnp.max(m_ref[pair % 2], axis=1, keepdims=True), (rows, LANES))
        acc_ref[...] = jnp.dot(probs(last_buf[pair % 2], pair), v_ref[pl.ds(last0, span), :],
                               preferred_element_type=F32)

    def pv_step(pair, t):
        k0 = pl.multiple_of(t * span, span)
        acc_ref[...] += jnp.dot(probs(s_buf[:, pl.ds(k0, span)], pair), v_ref[pl.ds(k0, span), :],
                                preferred_element_type=F32)

    n_pairs = group // MOBA_ROW_HEADS

    def sweep(step):
        def body(tt, carry):
            for u in range(MOBA_UNROLL):
                step(tt * MOBA_UNROLL + u)
            return carry

        lax.fori_loop(0, n_full // MOBA_UNROLL, body, 0)

        def tail(t, carry):
            step(t)
            return carry

        lax.fori_loop((n_full // MOBA_UNROLL) * MOBA_UNROLL, n_full, tail, 0)

    score_last(0)
    sweep(lambda t: score_step(0, t))
    for pair in range(n_pairs):
        pv_last(pair)
        if pair + 1 < n_pairs:
            score_last(pair + 1)

        def step(t, pair=pair):
            pv_step(pair, t)
            if pair + 1 < n_pairs:
                score_step(pair + 1, t)

        sweep(step)
        acc = acc_ref[...]
        o = acc[:, :HEAD_DIM] / acc[:, HEAD_DIM:HEAD_DIM + 1]
        for g in range(MOBA_ROW_HEADS):
            hd = pair * MOBA_ROW_HEADS + g
            o_ref[:, hd * HEAD_DIM:(hd + 1) * HEAD_DIM] = o[g * tq:(g + 1) * tq]


def _moba_prompt(qa, ka, va):
    n_heads, s, _ = qa.shape
    n_kv = ka.shape[0]
    group = n_heads // n_kv
    tq = MOBA_BLOCK
    assert group % MOBA_ROW_HEADS == 0 and s % MOBA_KEY_STEP == 0
    rows = MOBA_ROW_HEADS * tq
    whole_seq = pl.BlockSpec((None, s, LANES), lambda c, i: (c, 0, 0), pipeline_mode=pl.Buffered(1))
    return pl.pallas_call(
        _moba_prompt_kernel,
        grid=(n_kv, s // tq),
        in_specs=[pl.BlockSpec((group, tq, LANES), lambda c, i: (c, i, 0)), whole_seq, whole_seq],
        out_specs=pl.BlockSpec((tq, group * HEAD_DIM), lambda c, i: (i, c)),
        out_shape=jax.ShapeDtypeStruct((s, n_heads * HEAD_DIM), F32),
        scratch_shapes=[pltpu.VMEM((rows, s), F32), pltpu.VMEM((2, rows, MOBA_KEY_STEP), F32),
                        pltpu.VMEM((2, rows, LANES), F32), pltpu.VMEM((rows, LANES), F32)],
        compiler_params=pltpu.CompilerParams(dimension_semantics=("arbitrary", "arbitrary"),
                                             vmem_limit_bytes=VMEM_LIMIT),
        name="moba_prompt",
    )(qa, ka, va)


def _post_a_kernel(o_ref, gate_ref, x_ref, wo_ref, gkv_ref, wkv_ref, knb_ref, gb_ref, wb_ref, qnb_ref,
                   cos_ref, sin_ref, b2_ref, h_ref, kb_ref, vb_ref, qb_ref, gateb_ref):
    tm = x_ref.shape[0]
    b_width = qb_ref.shape[1]
    h = x_ref[...] + jnp.dot((o_ref[...] * _silu(gate_ref[...])).astype(BF16), wo_ref[...],
                             preferred_element_type=F32)
    h_ref[...] = h
    hn = h * _rms_scale(h)
    lane = lax.broadcasted_iota(jnp.int32, (tm, LANES), 1)
    first_half = (lane % HEAD_DIM) < HALF
    cos = cos_ref[...]
    sin = sin_ref[...]
    b2 = b2_ref[...]

    zkv = jnp.dot((hn * gkv_ref[...]).astype(BF16), wkv_ref[...], preferred_element_type=F32)
    kvw = kb_ref.shape[1]
    for c in range(kvw // LANES):
        kb_ref[:, c * LANES:(c + 1) * LANES] = _head_norm_rope(
            zkv[:, c * LANES:(c + 1) * LANES], knb_ref[...], cos, sin, b2, first_half)
    vb_ref[...] = zkv[:, kvw:]

    zb = jnp.dot((hn * gb_ref[...]).astype(BF16), wb_ref[...], preferred_element_type=F32)
    for c in range(b_width // LANES):
        ro = _head_norm_rope(zb[:, c * LANES:(c + 1) * LANES], qnb_ref[...], cos, sin, b2, first_half)
        qb_ref[:, c * LANES:(c + 1) * LANES] = (ro * SCALE).astype(BF16)
    gateb_ref[...] = zb[:, b_width:]


def _post_a(o, gate, x, wo, gkv, wkv, knb, gb, wb, qnb, cos, sin, b2, tm):
    s, d = x.shape
    kvw = wkv.shape[1] // 2
    b_width = wb.shape[1] // 2
    row = lambda width: pl.BlockSpec((tm, width), lambda i: (i, 0))
    consts = (wo, gkv, wkv, knb, gb, wb, qnb)
    return pl.pallas_call(
        _post_a_kernel,
        grid=(s // tm,),
        in_specs=[row(o.shape[1]), row(gate.shape[1]), row(d)] + [_const_spec(a.shape) for a in consts]
                 + [row(LANES), row(LANES), _const_spec(b2.shape)],
        out_specs=[row(d), row(kvw), row(kvw), row(b_width), row(b_width)],
        out_shape=[jax.ShapeDtypeStruct((s, d), F32), jax.ShapeDtypeStruct((s, kvw), F32),
                   jax.ShapeDtypeStruct((s, kvw), F32), jax.ShapeDtypeStruct((s, b_width), BF16),
                   jax.ShapeDtypeStruct((s, b_width), F32)],
        compiler_params=pltpu.CompilerParams(dimension_semantics=("parallel",),
                                             vmem_limit_bytes=VMEM_LIMIT),
        name="post_a",
    )(o, gate, x, *consts, cos, sin, b2)


def _swa_prompt_kernel(q_ref, kc_ref, kp_ref, vc_ref, vp_ref, sink_ref, gate_ref, h_ref, wo_ref,
                       y_ref, o_scr):
    n = pl.program_id(0)
    w = kp_ref.shape[0]
    sub = q_ref.shape[0] // w
    n_heads = q_ref.shape[1] // HEAD_DIM
    n_kv = kc_ref.shape[1] // HEAD_DIM
    group = n_heads // n_kv
    assert n_kv == HEADS_PER_VREG
    k_all = jnp.concatenate([kp_ref[...], kc_ref[...]], axis=0).astype(BF16)
    v_all = jnp.concatenate([vp_ref[...], vc_ref[...]], axis=0)
    cols = group * w
    j = lax.broadcasted_iota(jnp.int32, (2 * w, cols), 0)
    t = lax.broadcasted_iota(jnp.int32, (2 * w, cols), 1) % w
    band = (j >= t) & (j <= t + w)
    lane = lax.broadcasted_iota(jnp.int32, (w, LANES), 1)

    for u in range(sub):
        kk = k_all[u * w:(u + 2) * w]
        vt = v_all[u * w:(u + 2) * w].T.astype(BF16)
        mask = band & ((n > 0) | (j >= w)) if u == 0 else band
        for c in range(n_kv):
            in_slot = (lane // HEAD_DIM) == c
            parts = []
            sinks = []
            for g in range(group):
                hd = c * group + g
                col = (hd // HEADS_PER_VREG) * LANES
                vreg = q_ref[u * w:(u + 1) * w, col:col + LANES].astype(F32)
                if hd % HEADS_PER_VREG != c:
                    vreg = pltpu.roll(vreg, HEAD_DIM, 1)
                parts.append(jnp.where(in_slot, vreg, 0.0).astype(BF16))
                sinks.append(jnp.full((1, w), sink_ref[hd], F32))
            qs = jnp.concatenate(parts, axis=0)
            sk = jnp.concatenate(sinks, axis=1)
            st = jnp.where(mask, _nt(kk, qs), NEG_INF)
            m = jnp.maximum(jnp.max(st, axis=0, keepdims=True), sk)
            e = jnp.exp(st - m)
            inv = 1.0 / (jnp.sum(e, axis=0, keepdims=True) + jnp.exp(sk - m))
            ot = jnp.dot(vt, e.astype(BF16), preferred_element_type=F32) * inv
            oc = ot[c * HEAD_DIM:(c + 1) * HEAD_DIM]
            for g in range(0, group, HEADS_PER_VREG):
                hd = c * group + g
                both = jnp.concatenate([oc[:, g * w:(g + 1) * w], oc[:, (g + 1) * w:(g + 2) * w]], axis=0)
                dst = (hd // HEADS_PER_VREG) * LANES
                o_scr[u * w:(u + 1) * w, dst:dst + LANES] = both.T

    y_ref[...] = h_ref[...] + jnp.dot((o_scr[...] * _silu(gate_ref[...])).astype(BF16), wo_ref[...],
                                      preferred_element_type=F32)


def _swa_prompt(qb, kb, vb, sinks, gateb, h, wo):
    s, b_width = qb.shape
    kvw = kb.shape[1]
    d = h.shape[1]
    w = WINDOW
    tb = SWA_SUB * w
    cur = lambda width: pl.BlockSpec((tb, width), lambda n: (n, 0))
    prev = lambda width: pl.BlockSpec((w, width), lambda n: (jnp.maximum(n * SWA_SUB - 1, 0), 0))
    return pl.pallas_call(
        _swa_prompt_kernel,
        grid=(s // tb,),
        in_specs=[cur(b_width), cur(kvw), prev(kvw), cur(kvw), prev(kvw),
                  pl.BlockSpec(memory_space=pltpu.SMEM), cur(b_width), cur(d), _const_spec(wo.shape)],
        out_specs=cur(d),
        out_shape=jax.ShapeDtypeStruct((s, d), F32),
        scratch_shapes=[pltpu.VMEM((tb, b_width), F32)],
        compiler_params=pltpu.CompilerParams(dimension_semantics=("parallel",),
                                             vmem_limit_bytes=VMEM_LIMIT),
        name="swa_prompt",
    )(qb, kb, kb, vb, vb, sinks, gateb, h, wo)


def _moba_sample_kernel(pt_ref, wq_ref, knew_ref, vnew_ref, expand_ref, kc_ref, vc_ref,
                        o_ref, buf, sem, s_scr, p_scr, ksum_scr):
    b = pl.program_id(0)
    n_samples = pl.num_programs(0)
    n_heads = wq_ref.shape[1]
    kvw = wq_ref.shape[2]
    n_blocks = s_scr.shape[1] // MOBA_BLOCK
    blocks_per_chunk = PAGES_PER_CHUNK * PAGE_SIZE // MOBA_BLOCK
    assert MOBA_BLOCK == 2 * PAGE_SIZE and n_blocks <= LANES
    jobs = 2 * RING

    def page_copy(sample, job, p):
        cache = kc_ref if job < RING else vc_ref
        slot = job % RING
        page = pt_ref[sample, (job % RING) * PAGES_PER_CHUNK + p]
        return pltpu.make_async_copy(cache.at[page], buf.at[slot, p], sem.at[slot])

    def issue(sample, job):
        def body(p, carry):
            page_copy(sample, job, p).start()
            return carry
        lax.fori_loop(0, PAGES_PER_CHUNK, body, 0, unroll=8)

    def wait(sample, job):
        cache = kc_ref if job < RING else vc_ref
        slot = job % RING
        pltpu.make_async_copy(cache.at[pl.ds(0, PAGES_PER_CHUNK)], buf.at[slot], sem.at[slot]).wait()

    @pl.when(b == 0)
    def _():
        ksum_scr[...] = jnp.zeros(ksum_scr.shape, F32)
        for job in range(LOOKAHEAD):
            issue(b, job)

    def prefetch(job):
        nxt = job + LOOKAHEAD
        if nxt < jobs:
            issue(b, nxt)
        else:
            @pl.when(b + 1 < n_samples)
            def _():
                issue(b + 1, nxt - jobs)

    wq = wq_ref[0]
    wqs = (wq * SCALE).astype(BF16)

    lane_k = lax.broadcasted_iota(jnp.int32, (kvw, LANES), 1)
    for job in range(RING):
        wait(b, job)
        prefetch(job)

        def k_body(r, carry, job=job):
            kt0 = buf[job, 2 * r]
            kt1 = buf[job, 2 * r + 1]
            blk = job * blocks_per_chunk + r
            kt = jnp.concatenate([kt0, kt1], axis=1).astype(BF16)
            off = pl.multiple_of(blk * MOBA_BLOCK, MOBA_BLOCK)
            s_scr[:, pl.ds(off, MOBA_BLOCK)] = jnp.dot(wqs, kt, preferred_element_type=F32)
            col = jnp.sum(kt0 + kt1, axis=1, keepdims=True)
            ksum_scr[...] = jnp.where(lane_k == blk, col, ksum_scr[...])
            return carry

        lax.fori_loop(0, blocks_per_chunk, k_body, 0, unroll=4)

    kmean = ksum_scr[...] * (1.0 / MOBA_BLOCK)
    wq_hi, wq_lo = _split(wq)
    km_hi, km_lo = _split(kmean)
    dot = functools.partial(jnp.dot, preferred_element_type=F32)
    gate = dot(wq_hi, km_hi) + dot(wq_hi, km_lo) + dot(wq_lo, km_hi)
    lane = lax.broadcasted_iota(jnp.int32, gate.shape, 1)
    keep = _top3_keep(gate, lane < n_blocks, lane)
    bias = jnp.where(keep, 0.0, NEG_INF).astype(BF16)
    s = s_scr[...] + dot(bias, expand_ref[...])
    s_self = jnp.sum(wq * knew_ref[0], axis=1, keepdims=True) * SCALE
    m = jnp.maximum(jnp.max(s, axis=1, keepdims=True), s_self)
    p = jnp.exp(s - m)
    p_self = jnp.exp(s_self - m)
    denom = jnp.sum(p, axis=1, keepdims=True) + p_self
    p_scr[...] = p.astype(BF16)

    acc = jnp.zeros((n_heads, kvw), F32)
    for job in range(RING, jobs):
        wait(b, job)
        prefetch(job)

        def v_body(r, acc, job=job):
            vt = jnp.concatenate([buf[job % RING, 2 * r], buf[job % RING, 2 * r + 1]], axis=1)
            off = pl.multiple_of(((job - RING) * blocks_per_chunk + r) * MOBA_BLOCK, MOBA_BLOCK)
            return acc + _nt(p_scr[:, pl.ds(off, MOBA_BLOCK)], vt.astype(BF16))

        acc = lax.fori_loop(0, blocks_per_chunk, v_body, acc, unroll=4)

    o = (acc + p_self * vnew_ref[0]) / denom
    head = lax.broadcasted_iota(jnp.int32, (n_heads, HEAD_DIM), 0)
    group = n_heads // (kvw // HEAD_DIM)
    out = jnp.zeros((n_heads, HEAD_DIM), F32)
    for c in range(kvw // HEAD_DIM):
        out = jnp.where(head // group == c, o[:, c * HEAD_DIM:(c + 1) * HEAD_DIM], out)
    o_ref[0] = out


def _moba_sample(page_table, wq, knew, vnew, cache_kt, cache_vt):
    n, n_heads, kvw = wq.shape
    n_pages = page_table.shape[1]
    assert n_pages == RING * PAGES_PER_CHUNK and cache_kt.shape[1:] == (kvw, PAGE_SIZE)
    n_keys = n_pages * PAGE_SIZE
    expand = (jnp.arange(LANES)[:, None] == jnp.arange(n_keys)[None, :] // MOBA_BLOCK).astype(BF16)
    grid_spec = pltpu.PrefetchScalarGridSpec(
        num_scalar_prefetch=1,
        grid=(n,),
        in_specs=[pl.BlockSpec((1, n_heads, kvw), lambda b, pt: (b, 0, 0)),
                  pl.BlockSpec((1, 1, kvw), lambda b, pt: (b, 0, 0)),
                  pl.BlockSpec((1, 1, kvw), lambda b, pt: (b, 0, 0)),
                  pl.BlockSpec(expand.shape, lambda b, pt: (0, 0)),
                  pl.BlockSpec(memory_space=pl.ANY),
                  pl.BlockSpec(memory_space=pl.ANY)],
        out_specs=pl.BlockSpec((1, n_heads, HEAD_DIM), lambda b, pt: (b, 0, 0)),
        scratch_shapes=[pltpu.VMEM((RING, PAGES_PER_CHUNK, kvw, PAGE_SIZE), F32),
                        pltpu.SemaphoreType.DMA((RING,)),
                        pltpu.VMEM((n_heads, n_keys), F32),
                        pltpu.VMEM((n_heads, n_keys), BF16),
                        pltpu.VMEM((kvw, LANES), F32)],
    )
    return pl.pallas_call(
        _moba_sample_kernel,
        grid_spec=grid_spec,
        out_shape=jax.ShapeDtypeStruct((n, n_heads, HEAD_DIM), F32),
        compiler_params=pltpu.CompilerParams(dimension_semantics=("arbitrary",),
                                             vmem_limit_bytes=VMEM_LIMIT),
        name="moba_sample",
    )(page_table, wq, knew, vnew, expand, cache_kt, cache_vt)


def _swa_sample_kernel(wq_ref, sk_ref, sv_ref, knew_ref, vnew_ref, knewt_ref, vnewt_ref, sink_ref,
                       o_ref, wk_ref, wv_ref):
    nb, n_heads, kvw = wq_ref.shape
    w = sk_ref.shape[2]
    group = n_heads // (kvw // HEAD_DIM)
    head = lax.broadcasted_iota(jnp.int32, (n_heads, HEAD_DIM), 0)
    pos = lax.broadcasted_iota(jnp.int32, (kvw, w), 1)
    col_id = lax.broadcasted_iota(jnp.int32, knewt_ref.shape, 1)
    sink = sink_ref[...]
    for i in range(nb):
        wq = wq_ref[i]
        kt = sk_ref[i]
        vt = sv_ref[i]
        knew = knew_ref[i]
        vnew = vnew_ref[i]
        s = jnp.dot(wq, kt.astype(BF16), preferred_element_type=F32)
        s_new = jnp.sum(wq.astype(F32) * knew, axis=1, keepdims=True)
        m = jnp.maximum(jnp.maximum(jnp.max(s, axis=1, keepdims=True), s_new), sink)
        e = jnp.exp(s - m)
        e_new = jnp.exp(s_new - m)
        denom = jnp.sum(e, axis=1, keepdims=True) + e_new + jnp.exp(sink - m)
        o = _nt((e / denom).astype(BF16), vt.astype(BF16)) + (e_new / denom) * vnew
        out = jnp.zeros((n_heads, HEAD_DIM), F32)
        for c in range(kvw // HEAD_DIM):
            out = jnp.where(head // group == c, o[:, c * HEAD_DIM:(c + 1) * HEAD_DIM], out)
        o_ref[i] = out
        mine = col_id == pl.program_id(0) * nb + i
        knew_col = jnp.sum(jnp.where(mine, knewt_ref[...], 0.0), axis=1, keepdims=True)
        vnew_col = jnp.sum(jnp.where(mine, vnewt_ref[...], 0.0), axis=1, keepdims=True)
        wk_ref[i] = jnp.where(pos == w - 1, knew_col, pltpu.roll(kt, w - 1, 1))
        wv_ref[i] = jnp.where(pos == w - 1, vnew_col, pltpu.roll(vt, w - 1, 1))


def _swa_sample(wq, state_kt, state_vt, knew, vnew, sinks):
    n, n_heads, kvw = wq.shape
    w = state_kt.shape[2]
    nb = 8
    blk = lambda shape: pl.BlockSpec((nb,) + shape, lambda i: (i,) + (0,) * len(shape))
    knew_t = knew.T
    vnew_t = vnew.T
    return pl.pallas_call(
        _swa_sample_kernel,
        grid=(n // nb,),
        in_specs=[blk((n_heads, kvw)), blk((kvw, w)), blk((kvw, w)), blk((1, kvw)), blk((1, kvw)),
                  _const_spec(knew_t.shape), _const_spec(vnew_t.shape), _const_spec(sinks.shape)],
        out_specs=[blk((n_heads, HEAD_DIM)), blk((kvw, w)), blk((kvw, w))],
        out_shape=[jax.ShapeDtypeStruct((n, n_heads, HEAD_DIM), F32),
                   jax.ShapeDtypeStruct((n, kvw, w), F32), jax.ShapeDtypeStruct((n, kvw, w), F32)],
        compiler_params=pltpu.CompilerParams(dimension_semantics=("parallel",),
                                             vmem_limit_bytes=VMEM_LIMIT),
        name="swa_sample",
    )(wq, state_kt, state_vt, knew[:, None, :], vnew[:, None, :], knew_t, vnew_t, sinks)


def _gated_out_kernel(o_ref, gate_ref, h_ref, wo_ref, y_ref):
    y_ref[...] = h_ref[...] + jnp.dot((o_ref[...] * _silu(gate_ref[...])).astype(BF16), wo_ref[...],
                                      preferred_element_type=F32)


def _gated_out(o, gate, h, wo):
    return pl.pallas_call(
        _gated_out_kernel,
        grid=(1,),
        in_specs=[_const_spec(a.shape) for a in (o, gate, h, wo)],
        out_specs=_const_spec(h.shape),
        out_shape=jax.ShapeDtypeStruct(h.shape, F32),
        compiler_params=pltpu.CompilerParams(vmem_limit_bytes=VMEM_LIMIT),
        name="gated_out",
    )(o, gate, h, wo)


def _rope_tables(pos):
    inv = 1.0 / (ROPE_THETA ** (jnp.arange(HALF, dtype=F32) / HALF))
    ang = pos.astype(F32)[:, None] * inv[None, :]
    c, s = jnp.cos(ang), jnp.sin(ang)
    return jnp.tile(c, (1, LANES // HALF)), jnp.tile(jnp.concatenate([-s, s], axis=1), (1, HEADS_PER_VREG))


def _slot_rows(q, n_heads, n_kv):
    n = q.shape[0]
    group = n_heads // n_kv
    onehot = (jnp.arange(n_heads)[:, None] // group == jnp.arange(n_kv)[None, :]).astype(q.dtype)
    q4 = q.reshape(n, n_heads, 1, HEAD_DIM) * onehot[None, :, :, None]
    return q4.reshape(n, n_heads, n_kv * HEAD_DIM)


def kernel(x_prompt, x_sample, cache_a_k, cache_a_v, state_b_k, state_b_v, page_table, g_a, w_in_a, qn_a, kn_a, w_out_a, g_kv, w_kv, kn_b, g_b, w_in_b, qn_b, sinks_b, w_out_b):
    _, s, d = x_prompt.shape
    n, t, _ = x_sample.shape
    assert t == 1 and g_a.shape[0] == 1 and g_b.shape[0] == 1
    n_heads = w_out_a.shape[1] // HEAD_DIM
    b_heads = w_out_b.shape[1] // HEAD_DIM
    past = page_table.shape[1] * PAGE_SIZE
    n_pool = cache_a_k.shape[1]
    a_kvw = A_KV_HEADS * HEAD_DIM
    b_kvw = B_KV_HEADS * HEAD_DIM

    cos_p, sin_p = _rope_tables(jnp.arange(s, dtype=jnp.int32))
    cos_s, sin_s = _rope_tables(jnp.full((n,), past, jnp.int32))
    lane = jnp.arange(LANES)
    b2 = (lane[:, None] // HEAD_DIM == lane[None, :] // HEAD_DIM).astype(BF16)
    gain_a = jnp.concatenate([jnp.tile(qn_a[0], n_heads), jnp.tile(kn_a[0], A_KV_HEADS)])[None, :]
    w_a = w_in_a[0].astype(BF16)
    wo_a = w_out_a[0].astype(BF16)
    w_kvb = w_kv.astype(BF16)
    w_b = w_in_b[0].astype(BF16)
    wo_b = w_out_b[0].astype(BF16)
    knb = jnp.tile(kn_b, HEADS_PER_VREG)[None, :]
    qnb = jnp.tile(qn_b[0], HEADS_PER_VREG)[None, :]
    post_consts = (wo_a, g_kv[None, :], w_kvb, knb, g_b[0][None, :], w_b, qnb)

    xp = x_prompt[0]
    qa, ka, va, k_p, v_p, gate_p = _proj_a_prompt(xp, w_a, g_a[0][None, :], gain_a, cos_p, sin_p, b2,
                                                  n_heads, A_KV_HEADS)
    o_p = _moba_prompt(qa, ka, va)
    h_p, kb_p, vb_p, qb_p, gateb_p = _post_a(o_p, gate_p, xp, *post_consts, cos_p, sin_p, b2, MOBA_BLOCK)
    y_p = _swa_prompt(qb_p, kb_p, vb_p, sinks_b[0], gateb_p, h_p, wo_b)

    xs = x_sample[:, 0]
    q_s, k_s, v_s, gate_s = _proj_a_sample(xs, w_a, g_a[0][None, :], gain_a, cos_s, sin_s, b2,
                                           n_heads, A_KV_HEADS)
    feature_major = lambda a: jnp.moveaxis(a, -3, -1).reshape(a.shape[:-3] + (a.shape[-2] * HEAD_DIM, a.shape[-3]))
    o_s = _moba_sample(page_table, _slot_rows(q_s, n_heads, A_KV_HEADS), k_s[:, None, :], v_s[:, None, :],
                       feature_major(cache_a_k[0]), feature_major(cache_a_v[0]))
    h_s, kb_s, vb_s, qb_s, gateb_s = _post_a(o_s.reshape(n, n_heads * HEAD_DIM), gate_s, xs, *post_consts,
                                             cos_s, sin_s, b2, n)
    ob_s, win_skt, win_svt = _swa_sample(_slot_rows(qb_s, b_heads, B_KV_HEADS),
                                         feature_major(state_b_k), feature_major(state_b_v),
                                         kb_s, vb_s, sinks_b[0][:, None])
    y_s = _gated_out(ob_s.reshape(n, b_heads * HEAD_DIM), gateb_s, h_s, wo_b)
    token_major = lambda a: jnp.moveaxis(a.reshape(n, B_KV_HEADS, HEAD_DIM, WINDOW), -1, 1)
    win_sk, win_sv = token_major(win_skt), token_major(win_svt)

    return (y_p[None], y_s[:, None, :],
            k_p.reshape(1, 1, s, A_KV_HEADS, HEAD_DIM), v_p.reshape(1, 1, s, A_KV_HEADS, HEAD_DIM),
            kb_p[s - WINDOW:].reshape(1, WINDOW, B_KV_HEADS, HEAD_DIM),
            vb_p[s - WINDOW:].reshape(1, WINDOW, B_KV_HEADS, HEAD_DIM),
            k_s.reshape(1, n, 1, A_KV_HEADS, HEAD_DIM), v_s.reshape(1, n, 1, A_KV_HEADS, HEAD_DIM),
            win_sk, win_sv)
```

```python
import functools

import jax
import jax.numpy as jnp
from jax import lax
from jax.experimental import pallas as pl
from jax.experimental.pallas import tpu as pltpu

F32 = jnp.float32
BF16 = jnp.bfloat16

HEAD_DIM = 64
HALF = HEAD_DIM // 2
A_KV_HEADS = 4
B_KV_HEADS = 2
MOBA_BLOCK = 256
MOBA_TOPK = 3
PAGE_SIZE = 128
WINDOW = 128
ROPE_THETA = 10000.0
EPS = 1e-6
NEG_INF = -1e30
SCALE = HEAD_DIM ** -0.5

LANES = 128
HEADS_PER_VREG = LANES // HEAD_DIM
VMEM_LIMIT = 56 * 1024 * 1024

LOG2E = 1.4426950408889634
MOBA_KEY_STEP = 4 * MOBA_BLOCK
MOBA_ROW_HEADS = 2
MOBA_UNROLL = 4
SWA_SUB = 4
POST_A_ROWS = 512

PAGES_PER_CHUNK = 32
RING = 4
LOOKAHEAD = RING - 1


def _nt(a, b):
    return lax.dot_general(a, b, (((1,), (1,)), ((), ())), preferred_element_type=F32)


def _split(x):
    hi = x.astype(BF16)
    lo = (x - hi.astype(F32)).astype(BF16)
    return hi, lo


def _nt3(a_split, b_split):
    a_hi, a_lo = a_split
    b_hi, b_lo = b_split
    return _nt(a_hi, b_hi) + _nt(a_hi, b_lo) + _nt(a_lo, b_hi)


def _silu(g):
    return g * (1.0 / (1.0 + jnp.exp(-g)))


def _rms_scale(x):
    return lax.rsqrt(jnp.mean(x * x, axis=-1, keepdims=True) + EPS)


def _head_norm_rope(zz, gain, cos, sin, b2, first_half):
    ss = jnp.dot((zz * zz).astype(BF16), b2, preferred_element_type=F32)
    zh = zz * lax.rsqrt(ss * (1.0 / HEAD_DIM) + EPS) * gain
    partner = jnp.where(first_half, pltpu.roll(zh, LANES - HALF, 1), pltpu.roll(zh, HALF, 1))
    return zh * cos + partner * sin


def _top3_keep(g, valid, lane):
    cand = jnp.where(valid, g, NEG_INF)
    sel = jnp.zeros(g.shape, jnp.bool_)
    lane_f = lane.astype(F32)
    for _ in range(MOBA_TOPK):
        mx = jnp.max(cand, axis=1, keepdims=True)
        first = jnp.min(jnp.where(cand == mx, lane_f, 2.0 * LANES), axis=1, keepdims=True)
        pick = lane_f == first
        sel = sel | pick
        cand = jnp.where(pick, -jnp.inf, cand)
    return sel & valid


def _top3_keep_cols(g, valid, row):
    cand = jnp.where(valid, g, NEG_INF)
    sel = jnp.zeros(g.shape, jnp.bool_)
    row_f = row.astype(F32)
    for _ in range(MOBA_TOPK):
        mx = jnp.max(cand, axis=0, keepdims=True)
        first = jnp.min(jnp.where(cand == mx, row_f, 2.0 * LANES), axis=0, keepdims=True)
        pick = row_f == first
        sel = sel | pick
        cand = jnp.where(pick, -jnp.inf, cand)
    return sel & valid


def _proj_a_prompt_kernel(x_ref, w_ref, g_ref, gain_ref, cos_ref, sin_ref, b2_ref,
                          qa_ref, ka_ref, va_ref, k_ref, v_ref, gate_ref, kmt_ref,
                          *, n_heads, n_kv):
    i = pl.program_id(0)
    tm = x_ref.shape[0]
    a_width = n_heads * HEAD_DIM
    kv_width = n_kv * HEAD_DIM

    @pl.when(i == 0)
    def _():
        kmt_ref[...] = jnp.zeros(kmt_ref.shape, F32)

    x = x_ref[...]
    xb = (x * _rms_scale(x) * g_ref[...]).astype(BF16)
    z = jnp.dot(xb, w_ref[...], preferred_element_type=F32)

    lane = lax.broadcasted_iota(jnp.int32, (tm, LANES), 1)
    first_half = (lane % HEAD_DIM) < HALF
    low = lane < HEAD_DIM
    cos = cos_ref[...]
    sin = sin_ref[...]
    b2 = b2_ref[...]
    own = (lane - HEAD_DIM) == i
    blk = lax.broadcasted_iota(jnp.int32, (HEAD_DIM, tm), 0)
    pad_rows = jnp.full((LANES - HEAD_DIM, tm), NEG_INF, F32)

    group = n_heads // n_kv
    kmt_split = [[_split(kmt_ref[kvh, e]) for e in range(HEADS_PER_VREG)] for kvh in range(n_kv)]
    for c in range(a_width // LANES):
        ro = _head_norm_rope(z[:, c * LANES:(c + 1) * LANES], gain_ref[:, c * LANES:(c + 1) * LANES],
                             cos, sin, b2, first_half)
        ro_split = _split(ro)
        qs = ro * (SCALE * LOG2E)
        qs_r = pltpu.roll(qs, HEAD_DIM, 1)
        for e in range(HEADS_PER_VREG):
            gt = _nt3(kmt_split[(c * HEADS_PER_VREG) // group][e], ro_split)[HEAD_DIM:]
            keep = _top3_keep_cols(gt, blk < i, blk) | (blk == i)
            bias_t = jnp.concatenate([pad_rows, jnp.where(keep, 0.0, NEG_INF)], axis=0)
            qa = jnp.where(low, qs if e == 0 else qs_r, bias_t.T)
            qa_ref[c * HEADS_PER_VREG + e] = qa.astype(BF16)

    onehot = jnp.where(own, 1.0, 0.0)
    for c in range(kv_width // LANES):
        col = a_width + c * LANES
        ro = _head_norm_rope(z[:, col:col + LANES], gain_ref[:, col:col + LANES], cos, sin, b2, first_half)
        k_ref[:, c * LANES:(c + 1) * LANES] = ro
        ro_r = pltpu.roll(ro, HEAD_DIM, 1)
        cs = jnp.sum(ro, axis=0, keepdims=True) * (1.0 / MOBA_BLOCK)
        cs_r = jnp.sum(ro_r, axis=0, keepdims=True) * (1.0 / MOBA_BLOCK)
        low1 = low[0:1]
        for e in range(HEADS_PER_VREG):
            kvh = c * HEADS_PER_VREG + e
            ka_ref[kvh] = jnp.where(low, ro if e == 0 else ro_r, onehot).astype(BF16)
            kmt_ref[kvh, 0, pl.ds(HEAD_DIM + i, 1), :] = jnp.where(low1, cs if e == 0 else cs_r, 0.0)
            kmt_ref[kvh, 1, pl.ds(HEAD_DIM + i, 1), :] = jnp.where(low1, 0.0, cs_r if e == 0 else cs)

    ones_col = jnp.where(lane == HEAD_DIM, 1.0, 0.0)
    v0 = a_width + kv_width
    v_ref[...] = z[:, v0:v0 + kv_width]
    for c in range(kv_width // LANES):
        vv = z[:, v0 + c * LANES:v0 + (c + 1) * LANES]
        vv_r = pltpu.roll(vv, HEAD_DIM, 1)
        for e in range(HEADS_PER_VREG):
            va_ref[c * HEADS_PER_VREG + e] = jnp.where(low, vv if e == 0 else vv_r, ones_col).astype(BF16)

    gate_ref[...] = z[:, v0 + kv_width:]


def _proj_a_sample_kernel(x_ref, w_ref, g_ref, gain_ref, cos_ref, sin_ref, b2_ref,
                          q_ref, k_ref, v_ref, gate_ref, *, n_heads, n_kv):
    tm = x_ref.shape[0]
    a_width = n_heads * HEAD_DIM
    kv_width = n_kv * HEAD_DIM
    x = x_ref[...]
    xb = (x * _rms_scale(x) * g_ref[...]).astype(BF16)
    z = jnp.dot(xb, w_ref[...], preferred_element_type=F32)
    lane = lax.broadcasted_iota(jnp.int32, (tm, LANES), 1)
    first_half = (lane % HEAD_DIM) < HALF
    cos = cos_ref[...]
    sin = sin_ref[...]
    b2 = b2_ref[...]
    for c in range((a_width + kv_width) // LANES):
        ro = _head_norm_rope(z[:, c * LANES:(c + 1) * LANES], gain_ref[:, c * LANES:(c + 1) * LANES],
                             cos, sin, b2, first_half)
        if c * LANES < a_width:
            q_ref[:, c * LANES:(c + 1) * LANES] = ro
        else:
            k_ref[:, c * LANES - a_width:(c + 1) * LANES - a_width] = ro
    v0 = a_width + kv_width
    v_ref[...] = z[:, v0:v0 + kv_width]
    gate_ref[...] = z[:, v0 + kv_width:]


def _const_spec(shape):
    return pl.BlockSpec(shape, lambda *_: (0,) * len(shape))


def _proj_a_prompt(x, w, g, gain, cos, sin, b2, n_heads, n_kv):
    s, d = x.shape
    tm = MOBA_BLOCK
    n_in = w.shape[1]
    a_width = n_heads * HEAD_DIM
    kv_width = n_kv * HEAD_DIM
    row = lambda width: pl.BlockSpec((tm, width), lambda i: (i, 0))
    head_plane = lambda n: pl.BlockSpec((n, tm, LANES), lambda i: (0, i, 0))
    return pl.pallas_call(
        functools.partial(_proj_a_prompt_kernel, n_heads=n_heads, n_kv=n_kv),
        grid=(s // tm,),
        in_specs=[row(d), _const_spec((d, n_in)), _const_spec((1, d)), _const_spec((1, a_width + kv_width)),
                  row(LANES), row(LANES), _const_spec((LANES, LANES))],
        out_specs=[head_plane(n_heads), head_plane(n_kv), head_plane(n_kv),
                   row(kv_width), row(kv_width), row(a_width)],
        out_shape=[jax.ShapeDtypeStruct((n_heads, s, LANES), BF16),
                   jax.ShapeDtypeStruct((n_kv, s, LANES), BF16),
                   jax.ShapeDtypeStruct((n_kv, s, LANES), BF16),
                   jax.ShapeDtypeStruct((s, kv_width), F32),
                   jax.ShapeDtypeStruct((s, kv_width), F32),
                   jax.ShapeDtypeStruct((s, a_width), F32)],
        scratch_shapes=[pltpu.VMEM((n_kv, HEADS_PER_VREG, LANES, LANES), F32)],
        compiler_params=pltpu.CompilerParams(dimension_semantics=("arbitrary",),
                                             vmem_limit_bytes=VMEM_LIMIT),
        name="proj_a_prompt",
    )(x, w, g, gain, cos, sin, b2)


def _proj_a_sample(x, w, g, gain, cos, sin, b2, n_heads, n_kv):
    n, d = x.shape
    a_width = n_heads * HEAD_DIM
    kv_width = n_kv * HEAD_DIM
    full = lambda a: _const_spec(a.shape)
    out_shape = [jax.ShapeDtypeStruct((n, a_width), F32), jax.ShapeDtypeStruct((n, kv_width), F32),
                 jax.ShapeDtypeStruct((n, kv_width), F32), jax.ShapeDtypeStruct((n, a_width), F32)]
    return pl.pallas_call(
        functools.partial(_proj_a_sample_kernel, n_heads=n_heads, n_kv=n_kv),
        grid=(1,),
        in_specs=[full(a) for a in (x, w, g, gain, cos, sin, b2)],
        out_specs=[_const_spec(o.shape) for o in out_shape],
        out_shape=out_shape,
        compiler_params=pltpu.CompilerParams(vmem_limit_bytes=VMEM_LIMIT),
        name="proj_a_sample",
    )(x, w, g, gain, cos, sin, b2)


def _moba_prompt_kernel(q_ref, k_ref, v_ref, o_ref, s_buf, last_buf, m_ref, acc_ref):
    i = pl.program_id(1)
    group, tq, _ = q_ref.shape
    rows = MOBA_ROW_HEADS * tq
    span = MOBA_KEY_STEP
    n_full = (i * MOBA_BLOCK) // span
    last0 = pl.multiple_of(n_full * span, span)
    lane_groups = span // LANES

    def fold_max(s):
        parts = [s[:, c * LANES:(c + 1) * LANES] for c in range(lane_groups)]
        while len(parts) > 1:
            parts = [jnp.maximum(a, b) for a, b in zip(parts[::2], parts[1::2])]
        return parts[0]

    def q_rows(pair):
        return q_ref[pair * MOBA_ROW_HEADS:(pair + 1) * MOBA_ROW_HEADS].reshape(rows, LANES)

    def probs(s, pair):
        mb = m_ref[pair % 2]
        return jnp.concatenate([jnp.exp2(s[:, c * LANES:(c + 1) * LANES] - mb) for c in range(lane_groups)],
                               axis=1).astype(BF16)

    tok = lax.broadcasted_iota(jnp.int32, (rows, span), 0) % tq + i * tq
    key = lax.broadcasted_iota(jnp.int32, (rows, span), 1) + n_full * span
    causal = key <= tok

    def score_last(pair):
        s_last = jnp.where(causal, _nt(q_rows(pair), k_ref[pl.ds(last0, span), :]), NEG_INF)
        last_buf[pair % 2] = s_last
        m_ref[pair % 2] = fold_max(s_last)

    def score_step(pair, t):
        k0 = pl.multiple_of(t * span, span)
        s = _nt(q_rows(pair), k_ref[pl.ds(k0, span), :])
        s_buf[:, pl.ds(k0, span)] = s
        m_ref[pair % 2] = jnp.maximum(m_ref[pair % 2], fold_max(s))

    def pv_last(pair):
        m_ref[pair % 2] = jnp.broadcast_to(jnp.max(m_ref[pair % 2], axis=1, keepdims=True), (rows, LANES))
        acc_ref[...] = jnp.dot(probs(last_buf[pair % 2], pair), v_ref[pl.ds(last0, span), :],
                               preferred_element_type=F32)

    def pv_step(pair, t):
        k0 = pl.multiple_of(t * span, span)
        acc_ref[...] += jnp.dot(probs(s_buf[:, pl.ds(k0, span)], pair), v_ref[pl.ds(k0, span), :],
                                preferred_element_type=F32)

    n_pairs = group // MOBA_ROW_HEADS

    def sweep(step):
        def body(tt, carry):
            for u in range(MOBA_UNROLL):
                step(tt * MOBA_UNROLL + u)
            return carry

        lax.fori_loop(0, n_full // MOBA_UNROLL, body, 0)

        def tail(t, carry):
            step(t)
            return carry

        lax.fori_loop((n_full // MOBA_UNROLL) * MOBA_UNROLL, n_full, tail, 0)

    score_last(0)
    sweep(lambda t: score_step(0, t))
    for pair in range(n_pairs):
        pv_last(pair)
        if pair + 1 < n_pairs:
            score_last(pair + 1)

        def step(t, pair=pair):
            pv_step(pair, t)
            if pair + 1 < n_pairs:
                score_step(pair + 1, t)

        sweep(step)
        acc = acc_ref[...]
        o = acc[:, :HEAD_DIM] / acc[:, HEAD_DIM:HEAD_DIM + 1]
        for g in range(MOBA_ROW_HEADS):
            hd = pair * MOBA_ROW_HEADS + g
            o_ref[:, hd * HEAD_DIM:(hd + 1) * HEAD_DIM] = o[g * tq:(g + 1) * tq]


def _moba_prompt(qa, ka, va):
    n_heads, s, _ = qa.shape
    n_kv = ka.shape[0]
    group = n_heads // n_kv
    tq = MOBA_BLOCK
    assert group % MOBA_ROW_HEADS == 0 and s % MOBA_KEY_STEP == 0
    rows = MOBA_ROW_HEADS * tq
    whole_seq = pl.BlockSpec((None, s, LANES), lambda c, i: (c, 0, 0), pipeline_mode=pl.Buffered(1))
    return pl.pallas_call(
        _moba_prompt_kernel,
        grid=(n_kv, s // tq),
        in_specs=[pl.BlockSpec((group, tq, LANES), lambda c, i: (c, i, 0)), whole_seq, whole_seq],
        out_specs=pl.BlockSpec((tq, group * HEAD_DIM), lambda c, i: (i, c)),
        out_shape=jax.ShapeDtypeStruct((s, n_heads * HEAD_DIM), F32),
        scratch_shapes=[pltpu.VMEM((rows, s), F32), pltpu.VMEM((2, rows, MOBA_KEY_STEP), F32),
                        pltpu.VMEM((2, rows, LANES), F32), pltpu.VMEM((rows, LANES), F32)],
        compiler_params=pltpu.CompilerParams(dimension_semantics=("arbitrary", "arbitrary"),
                                             vmem_limit_bytes=VMEM_LIMIT),
        name="moba_prompt",
    )(qa, ka, va)


def _post_a_kernel(o_ref, gate_ref, x_ref, wo_ref, gkv_ref, wkv_ref, knb_ref, gb_ref, wb_ref, qnb_ref,
                   cos_ref, sin_ref, b2_ref, h_ref, kb_ref, vb_ref, qb_ref, gateb_ref):
    tm = x_ref.shape[0]
    b_width = qb_ref.shape[1]
    h = x_ref[...] + jnp.dot((o_ref[...] * _silu(gate_ref[...])).astype(BF16), wo_ref[...],
                             preferred_element_type=F32)
    h_ref[...] = h
    hn = h * _rms_scale(h)
    lane = lax.broadcasted_iota(jnp.int32, (tm, LANES), 1)
    first_half = (lane % HEAD_DIM) < HALF
    cos = cos_ref[...]
    sin = sin_ref[...]
    b2 = b2_ref[...]

    zkv = jnp.dot((hn * gkv_ref[...]).astype(BF16), wkv_ref[...], preferred_element_type=F32)
    kvw = kb_ref.shape[1]
    for c in range(kvw // LANES):
        kb_ref[:, c * LANES:(c + 1) * LANES] = _head_norm_rope(
            zkv[:, c * LANES:(c + 1) * LANES], knb_ref[...], cos, sin, b2, first_half)
    vb_ref[...] = zkv[:, kvw:]

    zb = jnp.dot((hn * gb_ref[...]).astype(BF16), wb_ref[...], preferred_element_type=F32)
    for c in range(b_width // LANES):
        ro = _head_norm_rope(zb[:, c * LANES:(c + 1) * LANES], qnb_ref[...], cos, sin, b2, first_half)
        qb_ref[:, c * LANES:(c + 1) * LANES] = (ro * SCALE).astype(BF16)
    gateb_ref[...] = zb[:, b_width:]


def _post_a(o, gate, x, wo, gkv, wkv, knb, gb, wb, qnb, cos, sin, b2, tm):
    s, d = x.shape
    kvw = wkv.shape[1] // 2
    b_width = wb.shape[1] // 2
    row = lambda width: pl.BlockSpec((tm, width), lambda i: (i, 0))
    consts = (wo, gkv, wkv, knb, gb, wb, qnb)
    return pl.pallas_call(
        _post_a_kernel,
        grid=(s // tm,),
        in_specs=[row(o.shape[1]), row(gate.shape[1]), row(d)] + [_const_spec(a.shape) for a in consts]
                 + [row(LANES), row(LANES), _const_spec(b2.shape)],
        out_specs=[row(d), row(kvw), row(kvw), row(b_width), row(b_width)],
        out_shape=[jax.ShapeDtypeStruct((s, d), F32), jax.ShapeDtypeStruct((s, kvw), F32),
                   jax.ShapeDtypeStruct((s, kvw), F32), jax.ShapeDtypeStruct((s, b_width), BF16),
                   jax.ShapeDtypeStruct((s, b_width), F32)],
        compiler_params=pltpu.CompilerParams(dimension_semantics=("parallel",),
                                             vmem_limit_bytes=VMEM_LIMIT),
        name="post_a",
    )(o, gate, x, *consts, cos, sin, b2)


def _swa_prompt_kernel(q_ref, kc_ref, kp_ref, vc_ref, vp_ref, sink_ref, gate_ref, h_ref, wo_ref,
                       y_ref, o_scr):
    n = pl.program_id(0)
    w = kp_ref.shape[0]
    sub = q_ref.shape[0] // w
    n_heads = q_ref.shape[1] // HEAD_DIM
    n_kv = kc_ref.shape[1] // HEAD_DIM
    group = n_heads // n_kv
    assert n_kv == HEADS_PER_VREG
    k_all = jnp.concatenate([kp_ref[...], kc_ref[...]], axis=0).astype(BF16)
    v_all = jnp.concatenate([vp_ref[...], vc_ref[...]], axis=0)
    cols = group * w
    j = lax.broadcasted_iota(jnp.int32, (2 * w, cols), 0)
    t = lax.broadcasted_iota(jnp.int32, (2 * w, cols), 1) % w
    band = (j >= t) & (j <= t + w)
    lane = lax.broadcasted_iota(jnp.int32, (w, LANES), 1)

    for u in range(sub):
        kk = k_all[u * w:(u + 2) * w]
        vt = v_all[u * w:(u + 2) * w].T.astype(BF16)
        mask = band & ((n > 0) | (j >= w)) if u == 0 else band
        for c in range(n_kv):
            in_slot = (lane // HEAD_DIM) == c
            parts = []
            sinks = []
            for g in range(group):
                hd = c * group + g
                col = (hd // HEADS_PER_VREG) * LANES
                vreg = q_ref[u * w:(u + 1) * w, col:col + LANES].astype(F32)
                if hd % HEADS_PER_VREG != c:
                    vreg = pltpu.roll(vreg, HEAD_DIM, 1)
                parts.append(jnp.where(in_slot, vreg, 0.0).astype(BF16))
                sinks.append(jnp.full((1, w), sink_ref[hd], F32))
            qs = jnp.concatenate(parts, axis=0)
            sk = jnp.concatenate(sinks, axis=1)
            st = jnp.where(mask, _nt(kk, qs), NEG_INF)
            m = jnp.maximum(jnp.max(st, axis=0, keepdims=True), sk)
            e = jnp.exp(st - m)
            inv = 1.0 / (jnp.sum(e, axis=0, keepdims=True) + jnp.exp(sk - m))
            ot = jnp.dot(vt, e.astype(BF16), preferred_element_type=F32) * inv
            oc = ot[c * HEAD_DIM:(c + 1) * HEAD_DIM]
            for g in range(0, group, HEADS_PER_VREG):
                hd = c * group + g
                both = jnp.concatenate([oc[:, g * w:(g + 1) * w], oc[:, (g + 1) * w:(g + 2) * w]], axis=0)
                dst = (hd // HEADS_PER_VREG) * LANES
                o_scr[u * w:(u + 1) * w, dst:dst + LANES] = both.T

    y_ref[...] = h_ref[...] + jnp.dot((o_scr[...] * _silu(gate_ref[...])).astype(BF16), wo_ref[...],
                                      preferred_element_type=F32)


def _swa_prompt(qb, kb, vb, sinks, gateb, h, wo):
    s, b_width = qb.shape
    kvw = kb.shape[1]
    d = h.shape[1]
    w = WINDOW
    tb = SWA_SUB * w
    cur = lambda width: pl.BlockSpec((tb, width), lambda n: (n, 0))
    prev = lambda width: pl.BlockSpec((w, width), lambda n: (jnp.maximum(n * SWA_SUB - 1, 0), 0))
    return pl.pallas_call(
        _swa_prompt_kernel,
        grid=(s // tb,),
        in_specs=[cur(b_width), cur(kvw), prev(kvw), cur(kvw), prev(kvw),
                  pl.BlockSpec(memory_space=pltpu.SMEM), cur(b_width), cur(d), _const_spec(wo.shape)],
        out_specs=cur(d),
        out_shape=jax.ShapeDtypeStruct((s, d), F32),
        scratch_shapes=[pltpu.VMEM((tb, b_width), F32)],
        compiler_params=pltpu.CompilerParams(dimension_semantics=("parallel",),
                                             vmem_limit_bytes=VMEM_LIMIT),
        name="swa_prompt",
    )(qb, kb, kb, vb, vb, sinks, gateb, h, wo)


def _moba_sample_kernel(pt_ref, wq_ref, knew_ref, vnew_ref, expand_ref, kc_ref, vc_ref,
                        o_ref, buf, sem, s_scr, p_scr, ksum_scr):
    b = pl.program_id(0)
    n_samples = pl.num_programs(0)
    n_heads = wq_ref.shape[1]
    kvw = wq_ref.shape[2]
    n_blocks = s_scr.shape[1] // MOBA_BLOCK
    blocks_per_chunk = PAGES_PER_CHUNK * PAGE_SIZE // MOBA_BLOCK
    assert MOBA_BLOCK == 2 * PAGE_SIZE and n_blocks <= LANES
    jobs = 2 * RING

    def page_copy(sample, job, p):
        cache = kc_ref if job < RING else vc_ref
        slot = job % RING
        page = pt_ref[sample, (job % RING) * PAGES_PER_CHUNK + p]
        return pltpu.make_async_copy(cache.at[page], buf.at[slot, p], sem.at[slot])

    def issue(sample, job):
        def body(p, carry):
            page_copy(sample, job, p).start()
            return carry
        lax.fori_loop(0, PAGES_PER_CHUNK, body, 0, unroll=8)

    def wait(sample, job):
        cache = kc_ref if job < RING else vc_ref
        slot = job % RING
        pltpu.make_async_copy(cache.at[pl.ds(0, PAGES_PER_CHUNK)], buf.at[slot], sem.at[slot]).wait()

    @pl.when(b == 0)
    def _():
        ksum_scr[...] = jnp.zeros(ksum_scr.shape, F32)
        for job in range(LOOKAHEAD):
            issue(b, job)

    def prefetch(job):
        nxt = job + LOOKAHEAD
        if nxt < jobs:
            issue(b, nxt)
        else:
            @pl.when(b + 1 < n_samples)
            def _():
                issue(b + 1, nxt - jobs)

    wq = wq_ref[0]
    wqs = (wq * SCALE).astype(BF16)

    lane_k = lax.broadcasted_iota(jnp.int32, (kvw, LANES), 1)
    for job in range(RING):
        wait(b, job)
        prefetch(job)

        def k_body(r, carry, job=job):
            kt0 = buf[job, 2 * r]
            kt1 = buf[job, 2 * r + 1]
            blk = job * blocks_per_chunk + r
            kt = jnp.concatenate([kt0, kt1], axis=1).astype(BF16)
            off = pl.multiple_of(blk * MOBA_BLOCK, MOBA_BLOCK)
            s_scr[:, pl.ds(off, MOBA_BLOCK)] = jnp.dot(wqs, kt, preferred_element_type=F32)
            col = jnp.sum(kt0 + kt1, axis=1, keepdims=True)
            ksum_scr[...] = jnp.where(lane_k == blk, col, ksum_scr[...])
            return carry

        lax.fori_loop(0, blocks_per_chunk, k_body, 0, unroll=4)

    kmean = ksum_scr[...] * (1.0 / MOBA_BLOCK)
    wq_hi, wq_lo = _split(wq)
    km_hi, km_lo = _split(kmean)
    dot = functools.partial(jnp.dot, preferred_element_type=F32)
    gate = dot(wq_hi, km_hi) + dot(wq_hi, km_lo) + dot(wq_lo, km_hi)
    lane = lax.broadcasted_iota(jnp.int32, gate.shape, 1)
    keep = _top3_keep(gate, lane < n_blocks, lane)
    bias = jnp.where(keep, 0.0, NEG_INF).astype(BF16)
    s = s_scr[...] + dot(bias, expand_ref[...])
    s_self = jnp.sum(wq * knew_ref[0], axis=1, keepdims=True) * SCALE
    m = jnp.maximum(jnp.max(s, axis=1, keepdims=True), s_self)
    p = jnp.exp(s - m)
    p_self = jnp.exp(s_self - m)
    denom = jnp.sum(p, axis=1, keepdims=True) + p_self
    p_scr[...] = p.astype(BF16)

    acc = jnp.zeros((n_heads, kvw), F32)
    for job in range(RING, jobs):
        wait(b, job)
        prefetch(job)

        def v_body(r, acc, job=job):
            vt = jnp.concatenate([buf[job % RING, 2 * r], buf[job % RING, 2 * r + 1]], axis=1)
            off = pl.multiple_of(((job - RING) * blocks_per_chunk + r) * MOBA_BLOCK, MOBA_BLOCK)
            return acc + _nt(p_scr[:, pl.ds(off, MOBA_BLOCK)], vt.astype(BF16))

        acc = lax.fori_loop(0, blocks_per_chunk, v_body, acc, unroll=4)

    o = (acc + p_self * vnew_ref[0]) / denom
    head = lax.broadcasted_iota(jnp.int32, (n_heads, HEAD_DIM), 0)
    group = n_heads // (kvw // HEAD_DIM)
    out = jnp.zeros((n_heads, HEAD_DIM), F32)
    for c in range(kvw // HEAD_DIM):
        out = jnp.where(head // group == c, o[:, c * HEAD_DIM:(c + 1) * HEAD_DIM], out)
    o_ref[0] = out


def _moba_sample(page_table, wq, knew, vnew, cache_kt, cache_vt):
    n, n_heads, kvw = wq.shape
    n_pages = page_table.shape[1]
    assert n_pages == RING * PAGES_PER_CHUNK and cache_kt.shape[1:] == (kvw, PAGE_SIZE)
    n_keys = n_pages * PAGE_SIZE
    expand = (jnp.arange(LANES)[:, None] == jnp.arange(n_keys)[None, :] // MOBA_BLOCK).astype(BF16)
    grid_spec = pltpu.PrefetchScalarGridSpec(
        num_scalar_prefetch=1,
        grid=(n,),
        in_specs=[pl.BlockSpec((1, n_heads, kvw), lambda b, pt: (b, 0, 0)),
                  pl.BlockSpec((1, 1, kvw), lambda b, pt: (b, 0, 0)),
                  pl.BlockSpec((1, 1, kvw), lambda b, pt: (b, 0, 0)),
                  pl.BlockSpec(expand.shape, lambda b, pt: (0, 0)),
                  pl.BlockSpec(memory_space=pl.ANY),
                  pl.BlockSpec(memory_space=pl.ANY)],
        out_specs=pl.BlockSpec((1, n_heads, HEAD_DIM), lambda b, pt: (b, 0, 0)),
        scratch_shapes=[pltpu.VMEM((RING, PAGES_PER_CHUNK, kvw, PAGE_SIZE), F32),
                        pltpu.SemaphoreType.DMA((RING,)),
                        pltpu.VMEM((n_heads, n_keys), F32),
                        pltpu.VMEM((n_heads, n_keys), BF16),
                        pltpu.VMEM((kvw, LANES), F32)],
    )
    return pl.pallas_call(
        _moba_sample_kernel,
        grid_spec=grid_spec,
        out_shape=jax.ShapeDtypeStruct((n, n_heads, HEAD_DIM), F32),
        compiler_params=pltpu.CompilerParams(dimension_semantics=("arbitrary",),
                                             vmem_limit_bytes=VMEM_LIMIT),
        name="moba_sample",
    )(page_table, wq, knew, vnew, expand, cache_kt, cache_vt)


def _swa_sample_kernel(wq_ref, sk_ref, sv_ref, knew_ref, vnew_ref, knewt_ref, vnewt_ref, sink_ref,
                       o_ref, wk_ref, wv_ref):
    nb, n_heads, kvw = wq_ref.shape
    w = sk_ref.shape[2]
    group = n_heads // (kvw // HEAD_DIM)
    head = lax.broadcasted_iota(jnp.int32, (n_heads, HEAD_DIM), 0)
    pos = lax.broadcasted_iota(jnp.int32, (kvw, w), 1)
    col_id = lax.broadcasted_iota(jnp.int32, knewt_ref.shape, 1)
    sink = sink_ref[...]
    for i in range(nb):
        wq = wq_ref[i]
        kt = sk_ref[i]
        vt = sv_ref[i]
        knew = knew_ref[i]
        vnew = vnew_ref[i]
        s = jnp.dot(wq, kt.astype(BF16), preferred_element_type=F32)
        s_new = jnp.sum(wq.astype(F32) * knew, axis=1, keepdims=True)
        m = jnp.maximum(jnp.maximum(jnp.max(s, axis=1, keepdims=True), s_new), sink)
        e = jnp.exp(s - m)
        e_new = jnp.exp(s_new - m)
        denom = jnp.sum(e, axis=1, keepdims=True) + e_new + jnp.exp(sink - m)
        o = _nt((e / denom).astype(BF16), vt.astype(BF16)) + (e_new / denom) * vnew
        out = jnp.zeros((n_heads, HEAD_DIM), F32)
        for c in range(kvw // HEAD_DIM):
            out = jnp.where(head // group == c, o[:, c * HEAD_DIM:(c + 1) * HEAD_DIM], out)
        o_ref[i] = out
        mine = col_id == pl.program_id(0) * nb + i
        knew_col = jnp.sum(jnp.where(mine, knewt_ref[...], 0.0), axis=1, keepdims=True)
        vnew_col = jnp.sum(jnp.where(mine, vnewt_ref[...], 0.0), axis=1, keepdims=True)
        wk_ref[i] = jnp.where(pos == w - 1, knew_col, pltpu.roll(kt, w - 1, 1))
        wv_ref[i] = jnp.where(pos == w - 1, vnew_col, pltpu.roll(vt, w - 1, 1))


def _swa_sample(wq, state_kt, state_vt, knew, vnew, sinks):
    n, n_heads, kvw = wq.shape
    w = state_kt.shape[2]
    nb = 8
    blk = lambda shape: pl.BlockSpec((nb,) + shape, lambda i: (i,) + (0,) * len(shape))
    knew_t = knew.T
    vnew_t = vnew.T
    return pl.pallas_call(
        _swa_sample_kernel,
        grid=(n // nb,),
        in_specs=[blk((n_heads, kvw)), blk((kvw, w)), blk((kvw, w)), blk((1, kvw)), blk((1, kvw)),
                  _const_spec(knew_t.shape), _const_spec(vnew_t.shape), _const_spec(sinks.shape)],
        out_specs=[blk((n_heads, HEAD_DIM)), blk((kvw, w)), blk((kvw, w))],
        out_shape=[jax.ShapeDtypeStruct((n, n_heads, HEAD_DIM), F32),
                   jax.ShapeDtypeStruct((n, kvw, w), F32), jax.ShapeDtypeStruct((n, kvw, w), F32)],
        compiler_params=pltpu.CompilerParams(dimension_semantics=("parallel",),
                                             vmem_limit_bytes=VMEM_LIMIT),
        name="swa_sample",
    )(wq, state_kt, state_vt, knew[:, None, :], vnew[:, None, :], knew_t, vnew_t, sinks)


def _gated_out_kernel(o_ref, gate_ref, h_ref, wo_ref, y_ref):
    y_ref[...] = h_ref[...] + jnp.dot((o_ref[...] * _silu(gate_ref[...])).astype(BF16), wo_ref[...],
                                      preferred_element_type=F32)


def _gated_out(o, gate, h, wo):
    return pl.pallas_call(
        _gated_out_kernel,
        grid=(1,),
        in_specs=[_const_spec(a.shape) for a in (o, gate, h, wo)],
        out_specs=_const_spec(h.shape),
        out_shape=jax.ShapeDtypeStruct(h.shape, F32),
        compiler_params=pltpu.CompilerParams(vmem_limit_bytes=VMEM_LIMIT),
        name="gated_out",
    )(o, gate, h, wo)


def _rope_tables(pos):
    inv = 1.0 / (ROPE_THETA ** (jnp.arange(HALF, dtype=F32) / HALF))
    ang = pos.astype(F32)[:, None] * inv[None, :]
    c, s = jnp.cos(ang), jnp.sin(ang)
    return jnp.tile(c, (1, LANES // HALF)), jnp.tile(jnp.concatenate([-s, s], axis=1), (1, HEADS_PER_VREG))


def _slot_rows(q, n_heads, n_kv):
    n = q.shape[0]
    group = n_heads // n_kv
    onehot = (jnp.arange(n_heads)[:, None] // group == jnp.arange(n_kv)[None, :]).astype(q.dtype)
    q4 = q.reshape(n, n_heads, 1, HEAD_DIM) * onehot[None, :, :, None]
    return q4.reshape(n, n_heads, n_kv * HEAD_DIM)


def kernel(x_prompt, x_sample, cache_a_k, cache_a_v, state_b_k, state_b_v, page_table, g_a, w_in_a, qn_a, kn_a, w_out_a, g_kv, w_kv, kn_b, g_b, w_in_b, qn_b, sinks_b, w_out_b):
    _, s, d = x_prompt.shape
    n, t, _ = x_sample.shape
    assert t == 1 and g_a.shape[0] == 1 and g_b.shape[0] == 1
    n_heads = w_out_a.shape[1] // HEAD_DIM
    b_heads = w_out_b.shape[1] // HEAD_DIM
    past = page_table.shape[1] * PAGE_SIZE
    n_pool = cache_a_k.shape[1]
    a_kvw = A_KV_HEADS * HEAD_DIM
    b_kvw = B_KV_HEADS * HEAD_DIM

    cos_p, sin_p = _rope_tables(jnp.arange(s, dtype=jnp.int32))
    cos_s, sin_s = _rope_tables(jnp.full((n,), past, jnp.int32))
    lane = jnp.arange(LANES)
    b2 = (lane[:, None] // HEAD_DIM == lane[None, :] // HEAD_DIM).astype(BF16)
    gain_a = jnp.concatenate([jnp.tile(qn_a[0], n_heads), jnp.tile(kn_a[0], A_KV_HEADS)])[None, :]
    w_a = w_in_a[0].astype(BF16)
    wo_a = w_out_a[0].astype(BF16)
    w_kvb = w_kv.astype(BF16)
    w_b = w_in_b[0].astype(BF16)
    wo_b = w_out_b[0].astype(BF16)
    knb = jnp.tile(kn_b, HEADS_PER_VREG)[None, :]
    qnb = jnp.tile(qn_b[0], HEADS_PER_VREG)[None, :]
    post_consts = (wo_a, g_kv[None, :], w_kvb, knb, g_b[0][None, :], w_b, qnb)

    xp = x_prompt[0]
    qa, ka, va, k_p, v_p, gate_p = _proj_a_prompt(xp, w_a, g_a[0][None, :], gain_a, cos_p, sin_p, b2,
                                                  n_heads, A_KV_HEADS)
    o_p = _moba_prompt(qa, ka, va)
    h_p, kb_p, vb_p, qb_p, gateb_p = _post_a(o_p, gate_p, xp, *post_consts, cos_p, sin_p, b2, POST_A_ROWS)
    y_p = _swa_prompt(qb_p, kb_p, vb_p, sinks_b[0], gateb_p, h_p, wo_b)

    xs = x_sample[:, 0]
    q_s, k_s, v_s, gate_s = _proj_a_sample(xs, w_a, g_a[0][None, :], gain_a, cos_s, sin_s, b2,
                                           n_heads, A_KV_HEADS)
    feature_major = lambda a: jnp.moveaxis(a, -3, -1).reshape(a.shape[:-3] + (a.shape[-2] * HEAD_DIM, a.shape[-3]))
    o_s = _moba_sample(page_table, _slot_rows(q_s, n_heads, A_KV_HEADS), k_s[:, None, :], v_s[:, None, :],
                       feature_major(cache_a_k[0]), feature_major(cache_a_v[0]))
    h_s, kb_s, vb_s, qb_s, gateb_s = _post_a(o_s.reshape(n, n_heads * HEAD_DIM), gate_s, xs, *post_consts,
                                             cos_s, sin_s, b2, n)
    ob_s, win_skt, win_svt = _swa_sample(_slot_rows(qb_s, b_heads, B_KV_HEADS),
                                         feature_major(state_b_k), feature_major(state_b_v),
                                         kb_s, vb_s, sinks_b[0][:, None])
    y_s = _gated_out(ob_s.reshape(n, b_heads * HEAD_DIM), gateb_s, h_s, wo_b)
    token_major = lambda a: jnp.moveaxis(a.reshape(n, B_KV_HEADS, HEAD_DIM, WINDOW), -1, 1)
    win_sk, win_sv = token_major(win_skt), token_major(win_svt)

    return (y_p[None], y_s[:, None, :],
            k_p.reshape(1, 1, s, A_KV_HEADS, HEAD_DIM), v_p.reshape(1, 1, s, A_KV_HEADS, HEAD_DIM),
            kb_p[s - WINDOW:].reshape(1, WINDOW, B_KV_HEADS, HEAD_DIM),
            vb_p[s - WINDOW:].reshape(1, WINDOW, B_KV_HEADS, HEAD_DIM),
            k_s.reshape(1, n, 1, A_KV_HEADS, HEAD_DIM), v_s.reshape(1, n, 1, A_KV_HEADS, HEAD_DIM),
            win_sk, win_sv)
```

```python
import functools

import jax
import jax.numpy as jnp
from jax import lax
from jax.experimental import pallas as pl
from jax.experimental.pallas import tpu as pltpu

F32 = jnp.float32
BF16 = jnp.bfloat16

HEAD_DIM = 64
HALF = HEAD_DIM // 2
A_KV_HEADS = 4
B_KV_HEADS = 2
MOBA_BLOCK = 256
MOBA_TOPK = 3
PAGE_SIZE = 128
WINDOW = 128
ROPE_THETA = 10000.0
EPS = 1e-6
NEG_INF = -1e30
SCALE = HEAD_DIM ** -0.5

LANES = 128
HEADS_PER_VREG = LANES // HEAD_DIM
VMEM_LIMIT = 56 * 1024 * 1024

LOG2E = 1.4426950408889634
MOBA_KEY_STEP = 4 * MOBA_BLOCK
MOBA_ROW_HEADS = 2
MOBA_UNROLL = 4
SWA_SUB = 4
POST_A_ROWS = 512

PAGES_PER_CHUNK = 32
RING = 4
LOOKAHEAD = RING - 1


def _nt(a, b):
    return lax.dot_general(a, b, (((1,), (1,)), ((), ())), preferred_element_type=F32)


def _split(x):
    hi = x.astype(BF16)
    lo = (x - hi.astype(F32)).astype(BF16)
    return hi, lo


def _nt3(a_split, b_split):
    a_hi, a_lo = a_split
    b_hi, b_lo = b_split
    return _nt(a_hi, b_hi) + _nt(a_hi, b_lo) + _nt(a_lo, b_hi)


def _silu(g):
    return g * (1.0 / (1.0 + jnp.exp(-g)))


def _rms_scale(x):
    return lax.rsqrt(jnp.mean(x * x, axis=-1, keepdims=True) + EPS)


def _head_norm_rope(zz, gain, cos, sin, b2, first_half):
    ss = jnp.dot((zz * zz).astype(BF16), b2, preferred_element_type=F32)
    zh = zz * lax.rsqrt(ss * (1.0 / HEAD_DIM) + EPS) * gain
    partner = jnp.where(first_half, pltpu.roll(zh, LANES - HALF, 1), pltpu.roll(zh, HALF, 1))
    return zh * cos + partner * sin


def _top3_keep(g, valid, lane):
    cand = jnp.where(valid, g, NEG_INF)
    sel = jnp.zeros(g.shape, jnp.bool_)
    lane_f = lane.astype(F32)
    for _ in range(MOBA_TOPK):
        mx = jnp.max(cand, axis=1, keepdims=True)
        first = jnp.min(jnp.where(cand == mx, lane_f, 2.0 * LANES), axis=1, keepdims=True)
        pick = lane_f == first
        sel = sel | pick
        cand = jnp.where(pick, -jnp.inf, cand)
    return sel & valid


def _top3_keep_cols(g, valid, row):
    cand = jnp.where(valid, g, NEG_INF)
    sel = jnp.zeros(g.shape, jnp.bool_)
    row_f = row.astype(F32)
    for _ in range(MOBA_TOPK):
        mx = jnp.max(cand, axis=0, keepdims=True)
        first = jnp.min(jnp.where(cand == mx, row_f, 2.0 * LANES), axis=0, keepdims=True)
        pick = row_f == first
        sel = sel | pick
        cand = jnp.where(pick, -jnp.inf, cand)
    return sel & valid


def _proj_a_prompt_kernel(x_ref, w_ref, g_ref, gain_ref, cos_ref, sin_ref, b2_ref,
                          qa_ref, ka_ref, va_ref, k_ref, v_ref, gate_ref, kmt_ref,
                          *, n_heads, n_kv):
    i = pl.program_id(0)
    tm = x_ref.shape[0]
    a_width = n_heads * HEAD_DIM
    kv_width = n_kv * HEAD_DIM

    @pl.when(i == 0)
    def _():
        kmt_ref[...] = jnp.zeros(kmt_ref.shape, F32)

    x = x_ref[...]
    xb = (x * _rms_scale(x) * g_ref[...]).astype(BF16)
    z = jnp.dot(xb, w_ref[...], preferred_element_type=F32)

    lane = lax.broadcasted_iota(jnp.int32, (tm, LANES), 1)
    first_half = (lane % HEAD_DIM) < HALF
    low = lane < HEAD_DIM
    cos = cos_ref[...]
    sin = sin_ref[...]
    b2 = b2_ref[...]
    own = (lane - HEAD_DIM) == i
    blk = lax.broadcasted_iota(jnp.int32, (HEAD_DIM, tm), 0)
    pad_rows = jnp.full((LANES - HEAD_DIM, tm), NEG_INF, F32)

    group = n_heads // n_kv
    kmt_split = [[_split(kmt_ref[kvh, e]) for e in range(HEADS_PER_VREG)] for kvh in range(n_kv)]
    for c in range(a_width // LANES):
        ro = _head_norm_rope(z[:, c * LANES:(c + 1) * LANES], gain_ref[:, c * LANES:(c + 1) * LANES],
                             cos, sin, b2, first_half)
        ro_split = _split(ro)
        qs = ro * (SCALE * LOG2E)
        qs_r = pltpu.roll(qs, HEAD_DIM, 1)
        for e in range(HEADS_PER_VREG):
            gt = _nt3(kmt_split[(c * HEADS_PER_VREG) // group][e], ro_split)[HEAD_DIM:]
            keep = _top3_keep_cols(gt, blk < i, blk) | (blk == i)
            bias_t = jnp.concatenate([pad_rows, jnp.where(keep, 0.0, NEG_INF)], axis=0)
            qa = jnp.where(low, qs if e == 0 else qs_r, bias_t.T)
            qa_ref[c * HEADS_PER_VREG + e] = qa.astype(BF16)

    onehot = jnp.where(own, 1.0, 0.0)
    for c in range(kv_width // LANES):
        col = a_width + c * LANES
        ro = _head_norm_rope(z[:, col:col + LANES], gain_ref[:, col:col + LANES], cos, sin, b2, first_half)
        k_ref[:, c * LANES:(c + 1) * LANES] = ro
        ro_r = pltpu.roll(ro, HEAD_DIM, 1)
        cs = jnp.sum(ro, axis=0, keepdims=True) * (1.0 / MOBA_BLOCK)
        cs_r = jnp.sum(ro_r, axis=0, keepdims=True) * (1.0 / MOBA_BLOCK)
        low1 = low[0:1]
        for e in range(HEADS_PER_VREG):
            kvh = c * HEADS_PER_VREG + e
            ka_ref[kvh] = jnp.where(low, ro if e == 0 else ro_r, onehot).astype(BF16)
            kmt_ref[kvh, 0, pl.ds(HEAD_DIM + i, 1), :] = jnp.where(low1, cs if e == 0 else cs_r, 0.0)
            kmt_ref[kvh, 1, pl.ds(HEAD_DIM + i, 1), :] = jnp.where(low1, 0.0, cs_r if e == 0 else cs)

    ones_col = jnp.where(lane == HEAD_DIM, 1.0, 0.0)
    v0 = a_width + kv_width
    v_ref[...] = z[:, v0:v0 + kv_width]
    for c in range(kv_width // LANES):
        vv = z[:, v0 + c * LANES:v0 + (c + 1) * LANES]
        vv_r = pltpu.roll(vv, HEAD_DIM, 1)
        for e in range(HEADS_PER_VREG):
            va_ref[c * HEADS_PER_VREG + e] = jnp.where(low, vv if e == 0 else vv_r, ones_col).astype(BF16)

    gate_ref[...] = z[:, v0 + kv_width:]


def _proj_a_sample_kernel(x_ref, w_ref, g_ref, gain_ref, cos_ref, sin_ref, b2_ref,
                          q_ref, k_ref, v_ref, gate_ref, *, n_heads, n_kv):
    tm = x_ref.shape[0]
    a_width = n_heads * HEAD_DIM
    kv_width = n_kv * HEAD_DIM
    x = x_ref[...]
    xb = (x * _rms_scale(x) * g_ref[...]).astype(BF16)
    z = jnp.dot(xb, w_ref[...], preferred_element_type=F32)
    lane = lax.broadcasted_iota(jnp.int32, (tm, LANES), 1)
    first_half = (lane % HEAD_DIM) < HALF
    cos = cos_ref[...]
    sin = sin_ref[...]
    b2 = b2_ref[...]
    for c in range((a_width + kv_width) // LANES):
        ro = _head_norm_rope(z[:, c * LANES:(c + 1) * LANES], gain_ref[:, c * LANES:(c + 1) * LANES],
                             cos, sin, b2, first_half)
        if c * LANES < a_width:
            q_ref[:, c * LANES:(c + 1) * LANES] = ro
        else:
            k_ref[:, c * LANES - a_width:(c + 1) * LANES - a_width] = ro
    v0 = a_width + kv_width
    v_ref[...] = z[:, v0:v0 + kv_width]
    gate_ref[...] = z[:, v0 + kv_width:]


def _const_spec(shape):
    return pl.BlockSpec(shape, lambda *_: (0,) * len(shape))


def _proj_a_prompt(x, w, g, gain, cos, sin, b2, n_heads, n_kv):
    s, d = x.shape
    tm = MOBA_BLOCK
    n_in = w.shape[1]
    a_width = n_heads * HEAD_DIM
    kv_width = n_kv * HEAD_DIM
    row = lambda width: pl.BlockSpec((tm, width), lambda i: (i, 0))
    head_plane = lambda n: pl.BlockSpec((n, tm, LANES), lambda i: (0, i, 0))
    return pl.pallas_call(
        functools.partial(_proj_a_prompt_kernel, n_heads=n_heads, n_kv=n_kv),
        grid=(s // tm,),
        in_specs=[row(d), _const_spec((d, n_in)), _const_spec((1, d)), _const_spec((1, a_width + kv_width)),
                  row(LANES), row(LANES), _const_spec((LANES, LANES))],
        out_specs=[head_plane(n_heads), head_plane(n_kv), head_plane(n_kv),
                   row(kv_width), row(kv_width), row(a_width)],
        out_shape=[jax.ShapeDtypeStruct((n_heads, s, LANES), BF16),
                   jax.ShapeDtypeStruct((n_kv, s, LANES), BF16),
                   jax.ShapeDtypeStruct((n_kv, s, LANES), BF16),
                   jax.ShapeDtypeStruct((s, kv_width), F32),
                   jax.ShapeDtypeStruct((s, kv_width), F32),
                   jax.ShapeDtypeStruct((s, a_width), F32)],
        scratch_shapes=[pltpu.VMEM((n_kv, HEADS_PER_VREG, LANES, LANES), F32)],
        compiler_params=pltpu.CompilerParams(dimension_semantics=("arbitrary",),
                                             vmem_limit_bytes=VMEM_LIMIT),
        name="proj_a_prompt",
    )(x, w, g, gain, cos, sin, b2)


def _proj_a_sample(x, w, g, gain, cos, sin, b2, n_heads, n_kv):
    n, d = x.shape
    a_width = n_heads * HEAD_DIM
    kv_width = n_kv * HEAD_DIM
    full = lambda a: _const_spec(a.shape)
    out_shape = [jax.ShapeDtypeStruct((n, a_width), F32), jax.ShapeDtypeStruct((n, kv_width), F32),
                 jax.ShapeDtypeStruct((n, kv_width), F32), jax.ShapeDtypeStruct((n, a_width), F32)]
    return pl.pallas_call(
        functools.partial(_proj_a_sample_kernel, n_heads=n_heads, n_kv=n_kv),
        grid=(1,),
        in_specs=[full(a) for a in (x, w, g, gain, cos, sin, b2)],
        out_specs=[_const_spec(o.shape) for o in out_shape],
        out_shape=out_shape,
        compiler_params=pltpu.CompilerParams(vmem_limit_bytes=VMEM_LIMIT),
        name="proj_a_sample",
    )(x, w, g, gain, cos, sin, b2)


def _moba_prompt_kernel(q_ref, k_ref, v_ref, o_ref, s_buf, last_buf, m_ref, acc_ref):
    i = pl.program_id(1)
    group, tq, _ = q_ref.shape
    rows = MOBA_ROW_HEADS * tq
    span = MOBA_KEY_STEP
    n_full = (i * MOBA_BLOCK) // span
    last0 = pl.multiple_of(n_full * span, span)
    lane_groups = span // LANES

    def fold_max(s):
        parts = [s[:, c * LANES:(c + 1) * LANES] for c in range(lane_groups)]
        while len(parts) > 1:
            parts = [jnp.maximum(a, b) for a, b in zip(parts[::2], parts[1::2])]
        return parts[0]

    def q_rows(pair):
        return q_ref[pair * MOBA_ROW_HEADS:(pair + 1) * MOBA_ROW_HEADS].reshape(rows, LANES)

    def probs(s, pair):
        mb = m_ref[pair % 2]
        return jnp.concatenate([jnp.exp2(s[:, c * LANES:(c + 1) * LANES] - mb) for c in range(lane_groups)],
                               axis=1).astype(BF16)

    tok = lax.broadcasted_iota(jnp.int32, (rows, span), 0) % tq + i * tq
    key = lax.broadcasted_iota(jnp.int32, (rows, span), 1) + n_full * span
    causal = key <= tok

    def score_last(pair):
        s_last = jnp.where(causal, _nt(q_rows(pair), k_ref[pl.ds(last0, span), :]), NEG_INF)
        last_buf[pair % 2] = s_last
        m_ref[pair % 2] = fold_max(s_last)

    def score_step(pair, t):
        k0 = pl.multiple_of(t * span, span)
        s = _nt(q_rows(pair), k_ref[pl.ds(k0, span), :])
        s_buf[:, pl.ds(k0, span)] = s
        m_ref[pair % 2] = jnp.maximum(m_ref[pair % 2], fold_max(s))

    def pv_last(pair):
        m_ref[pair % 2] = jnp.broadcast_to(jnp.max(m_ref[pair % 2], axis=1, keepdims=True), (rows, LANES))
        acc_ref[...] = jnp.dot(probs(last_buf[pair % 2], pair), v_ref[pl.ds(last0, span), :],
                               preferred_element_type=F32)

    def pv_step(pair, t):
        k0 = pl.multiple_of(t * span, span)
        acc_ref[...] += jnp.dot(probs(s_buf[:, pl.ds(k0, span)], pair), v_ref[pl.ds(k0, span), :],
                                preferred_element_type=F32)

    n_pairs = group // MOBA_ROW_HEADS

    def sweep(step):
        def body(tt, carry):
            for u in range(MOBA_UNROLL):
                step(tt * MOBA_UNROLL + u)
            return carry

        lax.fori_loop(0, n_full // MOBA_UNROLL, body, 0)

        def tail(t, carry):
            step(t)
            return carry

        lax.fori_loop((n_full // MOBA_UNROLL) * MOBA_UNROLL, n_full, tail, 0)

    score_last(0)
    sweep(lambda t: score_step(0, t))
    for pair in range(n_pairs):
        pv_last(pair)
        if pair + 1 < n_pairs:
            score_last(pair + 1)

        def step(t, pair=pair):
            pv_step(pair, t)
            if pair + 1 < n_pairs:
                score_step(pair + 1, t)

        sweep(step)
        acc = acc_ref[...]
        o = acc[:, :HEAD_DIM] / acc[:, HEAD_DIM:HEAD_DIM + 1]
        for g in range(MOBA_ROW_HEADS):
            hd = pair * MOBA_ROW_HEADS + g
            o_ref[:, hd * HEAD_DIM:(hd + 1) * HEAD_DIM] = o[g * tq:(g + 1) * tq]


def _moba_prompt(qa, ka, va):
    n_heads, s, _ = qa.shape
    n_kv = ka.shape[0]
    group = n_heads // n_kv
    tq = MOBA_BLOCK
    assert group % MOBA_ROW_HEADS == 0 and s % MOBA_KEY_STEP == 0
    rows = MOBA_ROW_HEADS * tq
    whole_seq = pl.BlockSpec((None, s, LANES), lambda c, i: (c, 0, 0), pipeline_mode=pl.Buffered(1))
    return pl.pallas_call(
        _moba_prompt_kernel,
        grid=(n_kv, s // tq),
        in_specs=[pl.BlockSpec((group, tq, LANES), lambda c, i: (c, i, 0)), whole_seq, whole_seq],
        out_specs=pl.BlockSpec((tq, group * HEAD_DIM), lambda c, i: (i, c)),
        out_shape=jax.ShapeDtypeStruct((s, n_heads * HEAD_DIM), F32),
        scratch_shapes=[pltpu.VMEM((rows, s), F32), pltpu.VMEM((2, rows, MOBA_KEY_STEP), F32),
                        pltpu.VMEM((2, rows, LANES), F32), pltpu.VMEM((rows, LANES), F32)],
        compiler_params=pltpu.CompilerParams(dimension_semantics=("arbitrary", "arbitrary"),
                                             vmem_limit_bytes=VMEM_LIMIT),
        name="moba_prompt",
    )(qa, ka, va)


def _post_a_kernel(o_ref, gate_ref, x_ref, wo_ref, gkv_ref, wkv_ref, knb_ref, gb_ref, wb_ref, qnb_ref,
                   cos_ref, sin_ref, b2_ref, h_ref, kb_ref, vb_ref, qb_ref, gateb_ref):
    tm = x_ref.shape[0]
    b_width = qb_ref.shape[1]
    h = x_ref[...] + jnp.dot((o_ref[...] * _silu(gate_ref[...])).astype(BF16), wo_ref[...],
                             preferred_element_type=F32)
    h_ref[...] = h
    hn = h * _rms_scale(h)
    lane = lax.broadcasted_iota(jnp.int32, (tm, LANES), 1)
    first_half = (lane % HEAD_DIM) < HALF
    cos = cos_ref[...]
    sin = sin_ref[...]
    b2 = b2_ref[...]

    zkv = jnp.dot((hn * gkv_ref[...]).astype(BF16), wkv_ref[...], preferred_element_type=F32)
    kvw = kb_ref.shape[1]
    for c in range(kvw // LANES):
        kb_ref[:, c * LANES:(c + 1) * LANES] = _head_norm_rope(
            zkv[:, c * LANES:(c + 1) * LANES], knb_ref[...], cos, sin, b2, first_half)
    vb_ref[...] = zkv[:, kvw:]

    zb = jnp.dot((hn * gb_ref[...]).astype(BF16), wb_ref[...], preferred_element_type=F32)
    for c in range(b_width // LANES):
        ro = _head_norm_rope(zb[:, c * LANES:(c + 1) * LANES], qnb_ref[...], cos, sin, b2, first_half)
        qb_ref[:, c * LANES:(c + 1) * LANES] = (ro * SCALE).astype(BF16)
    gateb_ref[...] = zb[:, b_width:]


def _post_a(o, gate, x, wo, gkv, wkv, knb, gb, wb, qnb, cos, sin, b2, tm):
    s, d = x.shape
    kvw = wkv.shape[1] // 2
    b_width = wb.shape[1] // 2
    row = lambda width: pl.BlockSpec((tm, width), lambda i: (i, 0))
    consts = (wo, gkv, wkv, knb, gb, wb, qnb)
    return pl.pallas_call(
        _post_a_kernel,
        grid=(s // tm,),
        in_specs=[row(o.shape[1]), row(gate.shape[1]), row(d)] + [_const_spec(a.shape) for a in consts]
                 + [row(LANES), row(LANES), _const_spec(b2.shape)],
        out_specs=[row(d), row(kvw), row(kvw), row(b_width), row(b_width)],
        out_shape=[jax.ShapeDtypeStruct((s, d), F32), jax.ShapeDtypeStruct((s, kvw), F32),
                   jax.ShapeDtypeStruct((s, kvw), F32), jax.ShapeDtypeStruct((s, b_width), BF16),
                   jax.ShapeDtypeStruct((s, b_width), F32)],
        compiler_params=pltpu.CompilerParams(dimension_semantics=("parallel",),
                                             vmem_limit_bytes=VMEM_LIMIT),
        name="post_a",
    )(o, gate, x, *consts, cos, sin, b2)


def _swa_prompt_kernel(q_ref, kc_ref, kp_ref, vc_ref, vp_ref, sink_ref, gate_ref, h_ref, wo_ref,
                       y_ref, o_scr):
    n = pl.program_id(0)
    w = kp_ref.shape[0]
    sub = q_ref.shape[0] // w
    n_heads = q_ref.shape[1] // HEAD_DIM
    n_kv = kc_ref.shape[1] // HEAD_DIM
    group = n_heads // n_kv
    assert n_kv == HEADS_PER_VREG
    k_all = jnp.concatenate([kp_ref[...], kc_ref[...]], axis=0).astype(BF16)
    v_all = jnp.concatenate([vp_ref[...], vc_ref[...]], axis=0)
    cols = group * w
    j = lax.broadcasted_iota(jnp.int32, (2 * w, cols), 0)
    t = lax.broadcasted_iota(jnp.int32, (2 * w, cols), 1) % w
    band = (j >= t) & (j <= t + w)
    lane = lax.broadcasted_iota(jnp.int32, (w, LANES), 1)

    for u in range(sub):
        kk = k_all[u * w:(u + 2) * w]
        vt = v_all[u * w:(u + 2) * w].T.astype(BF16)
        mask = band & ((n > 0) | (j >= w)) if u == 0 else band
        for c in range(n_kv):
            in_slot = (lane // HEAD_DIM) == c
            parts = []
            sinks = []
            for g in range(group):
                hd = c * group + g
                col = (hd // HEADS_PER_VREG) * LANES
                vreg = q_ref[u * w:(u + 1) * w, col:col + LANES].astype(F32)
                if hd % HEADS_PER_VREG != c:
                    vreg = pltpu.roll(vreg, HEAD_DIM, 1)
                parts.append(jnp.where(in_slot, vreg, 0.0).astype(BF16))
                sinks.append(jnp.full((1, w), sink_ref[hd], F32))
            qs = jnp.concatenate(parts, axis=0)
            sk = jnp.concatenate(sinks, axis=1)
            st = jnp.where(mask, _nt(kk, qs), NEG_INF)
            m = jnp.maximum(jnp.max(st, axis=0, keepdims=True), sk)
            e = jnp.exp(st - m)
            inv = 1.0 / (jnp.sum(e, axis=0, keepdims=True) + jnp.exp(sk - m))
            ot = jnp.dot(vt, e.astype(BF16), preferred_element_type=F32) * inv
            oc = ot[c * HEAD_DIM:(c + 1) * HEAD_DIM]
            for g in range(0, group, HEADS_PER_VREG):
                hd = c * group + g
                both = jnp.concatenate([oc[:, g * w:(g + 1) * w], oc[:, (g + 1) * w:(g + 2) * w]], axis=0)
                dst = (hd // HEADS_PER_VREG) * LANES
                o_scr[u * w:(u + 1) * w, dst:dst + LANES] = both.T

    y_ref[...] = h_ref[...] + jnp.dot((o_scr[...] * _silu(gate_ref[...])).astype(BF16), wo_ref[...],
                                      preferred_element_type=F32)


def _swa_prompt(qb, kb, vb, sinks, gateb, h, wo):
    s, b_width = qb.shape
    kvw = kb.shape[1]
    d = h.shape[1]
    w = WINDOW
    tb = SWA_SUB * w
    cur = lambda width: pl.BlockSpec((tb, width), lambda n: (n, 0))
    prev = lambda width: pl.BlockSpec((w, width), lambda n: (jnp.maximum(n * SWA_SUB - 1, 0), 0))
    return pl.pallas_call(
        _swa_prompt_kernel,
        grid=(s // tb,),
        in_specs=[cur(b_width), cur(kvw), prev(kvw), cur(kvw), prev(kvw),
                  pl.BlockSpec(memory_space=pltpu.SMEM), cur(b_width), cur(d), _const_spec(wo.shape)],
        out_specs=cur(d),
        out_shape=jax.ShapeDtypeStruct((s, d), F32),
        scratch_shapes=[pltpu.VMEM((tb, b_width), F32)],
        compiler_params=pltpu.CompilerParams(dimension_semantics=("parallel",),
                                             vmem_limit_bytes=VMEM_LIMIT),
        name="swa_prompt",
    )(qb, kb, kb, vb, vb, sinks, gateb, h, wo)


def _moba_sample_kernel(pt_ref, wq_ref, knew_ref, vnew_ref, expand_ref, kc_ref, vc_ref,
                        o_ref, buf, sem, s_scr, p_scr, ksum_scr):
    b = pl.program_id(0)
    n_samples = pl.num_programs(0)
    n_heads = wq_ref.shape[1]
    kvw = wq_ref.shape[2]
    n_blocks = s_scr.shape[1] // MOBA_BLOCK
    blocks_per_chunk = PAGES_PER_CHUNK * PAGE_SIZE // MOBA_BLOCK
    assert MOBA_BLOCK == 2 * PAGE_SIZE and n_blocks <= LANES
    jobs = 2 * RING

    def page_copy(sample, job, p):
        cache = kc_ref if job < RING else vc_ref
        slot = job % RING
        page = pt_ref[sample, (job % RING) * PAGES_PER_CHUNK + p]
        return pltpu.make_async_copy(cache.at[page], buf.at[slot, p], sem.at[slot])

    def issue(sample, job):
        def body(p, carry):
            page_copy(sample, job, p).start()
            return carry
        lax.fori_loop(0, PAGES_PER_CHUNK, body, 0, unroll=8)

    def wait(sample, job):
        cache = kc_ref if job < RING else vc_ref
        slot = job % RING
        pltpu.make_async_copy(cache.at[pl.ds(0, PAGES_PER_CHUNK)], buf.at[slot], sem.at[slot]).wait()

    @pl.when(b == 0)
    def _():
        ksum_scr[...] = jnp.zeros(ksum_scr.shape, F32)
        for job in range(LOOKAHEAD):
            issue(b, job)

    def prefetch(job):
        nxt = job + LOOKAHEAD
        if nxt < jobs:
            issue(b, nxt)
        else:
            @pl.when(b + 1 < n_samples)
            def _():
                issue(b + 1, nxt - jobs)

    wq = wq_ref[0]
    wqs = (wq * SCALE).astype(BF16)

    lane_k = lax.broadcasted_iota(jnp.int32, (kvw, LANES), 1)
    for job in range(RING):
        wait(b, job)
        prefetch(job)

        def k_body(r, carry, job=job):
            kt0 = buf[job, 2 * r]
            kt1 = buf[job, 2 * r + 1]
            blk = job * blocks_per_chunk + r
            kt = jnp.concatenate([kt0, kt1], axis=1).astype(BF16)
            off = pl.multiple_of(blk * MOBA_BLOCK, MOBA_BLOCK)
            s_scr[:, pl.ds(off, MOBA_BLOCK)] = jnp.dot(wqs, kt, preferred_element_type=F32)
            col = jnp.sum(kt0 + kt1, axis=1, keepdims=True)
            ksum_scr[...] = jnp.where(lane_k == blk, col, ksum_scr[...])
            return carry

        lax.fori_loop(0, blocks_per_chunk, k_body, 0, unroll=True)

    kmean = ksum_scr[...] * (1.0 / MOBA_BLOCK)
    wq_hi, wq_lo = _split(wq)
    km_hi, km_lo = _split(kmean)
    dot = functools.partial(jnp.dot, preferred_element_type=F32)
    gate = dot(wq_hi, km_hi) + dot(wq_hi, km_lo) + dot(wq_lo, km_hi)
    lane = lax.broadcasted_iota(jnp.int32, gate.shape, 1)
    keep = _top3_keep(gate, lane < n_blocks, lane)
    bias = jnp.where(keep, 0.0, NEG_INF).astype(BF16)
    s = s_scr[...] + dot(bias, expand_ref[...])
    s_self = jnp.sum(wq * knew_ref[0], axis=1, keepdims=True) * SCALE
    m = jnp.maximum(jnp.max(s, axis=1, keepdims=True), s_self)
    p = jnp.exp(s - m)
    p_self = jnp.exp(s_self - m)
    denom = jnp.sum(p, axis=1, keepdims=True) + p_self
    p_scr[...] = p.astype(BF16)

    acc = jnp.zeros((n_heads, kvw), F32)
    for job in range(RING, jobs):
        wait(b, job)
        prefetch(job)

        def v_body(r, acc, job=job):
            vt = jnp.concatenate([buf[job % RING, 2 * r], buf[job % RING, 2 * r + 1]], axis=1)
            off = pl.multiple_of(((job - RING) * blocks_per_chunk + r) * MOBA_BLOCK, MOBA_BLOCK)
            return acc + _nt(p_scr[:, pl.ds(off, MOBA_BLOCK)], vt.astype(BF16))

        acc = lax.fori_loop(0, blocks_per_chunk, v_body, acc, unroll=True)

    o = (acc + p_self * vnew_ref[0]) / denom
    head = lax.broadcasted_iota(jnp.int32, (n_heads, HEAD_DIM), 0)
    group = n_heads // (kvw // HEAD_DIM)
    out = jnp.zeros((n_heads, HEAD_DIM), F32)
    for c in range(kvw // HEAD_DIM):
        out = jnp.where(head // group == c, o[:, c * HEAD_DIM:(c + 1) * HEAD_DIM], out)
    o_ref[0] = out


def _moba_sample(page_table, wq, knew, vnew, cache_kt, cache_vt):
    n, n_heads, kvw = wq.shape
    n_pages = page_table.shape[1]
    assert n_pages == RING * PAGES_PER_CHUNK and cache_kt.shape[1:] == (kvw, PAGE_SIZE)
    n_keys = n_pages * PAGE_SIZE
    expand = (jnp.arange(LANES)[:, None] == jnp.arange(n_keys)[None, :] // MOBA_BLOCK).astype(BF16)
    grid_spec = pltpu.PrefetchScalarGridSpec(
        num_scalar_prefetch=1,
        grid=(n,),
        in_specs=[pl.BlockSpec((1, n_heads, kvw), lambda b, pt: (b, 0, 0)),
                  pl.BlockSpec((1, 1, kvw), lambda b, pt: (b, 0, 0)),
                  pl.BlockSpec((1, 1, kvw), lambda b, pt: (b, 0, 0)),
                  pl.BlockSpec(expand.shape, lambda b, pt: (0, 0)),
                  pl.BlockSpec(memory_space=pl.ANY),
                  pl.BlockSpec(memory_space=pl.ANY)],
        out_specs=pl.BlockSpec((1, n_heads, HEAD_DIM), lambda b, pt: (b, 0, 0)),
        scratch_shapes=[pltpu.VMEM((RING, PAGES_PER_CHUNK, kvw, PAGE_SIZE), F32),
                        pltpu.SemaphoreType.DMA((RING,)),
                        pltpu.VMEM((n_heads, n_keys), F32),
                        pltpu.VMEM((n_heads, n_keys), BF16),
                        pltpu.VMEM((kvw, LANES), F32)],
    )
    return pl.pallas_call(
        _moba_sample_kernel,
        grid_spec=grid_spec,
        out_shape=jax.ShapeDtypeStruct((n, n_heads, HEAD_DIM), F32),
        compiler_params=pltpu.CompilerParams(dimension_semantics=("arbitrary",),
                                             vmem_limit_bytes=VMEM_LIMIT),
        name="moba_sample",
    )(page_table, wq, knew, vnew, expand, cache_kt, cache_vt)


def _swa_sample_kernel(wq_ref, sk_ref, sv_ref, knew_ref, vnew_ref, knewt_ref, vnewt_ref, sink_ref,
                       o_ref, wk_ref, wv_ref):
    nb, n_heads, kvw = wq_ref.shape
    w = sk_ref.shape[2]
    group = n_heads // (kvw // HEAD_DIM)
    head = lax.broadcasted_iota(jnp.int32, (n_heads, HEAD_DIM), 0)
    pos = lax.broadcasted_iota(jnp.int32, (kvw, w), 1)
    col_id = lax.broadcasted_iota(jnp.int32, knewt_ref.shape, 1)
    sink = sink_ref[...]
    for i in range(nb):
        wq = wq_ref[i]
        kt = sk_ref[i]
        vt = sv_ref[i]
        knew = knew_ref[i]
        vnew = vnew_ref[i]
        s = jnp.dot(wq, kt.astype(BF16), preferred_element_type=F32)
        s_new = jnp.sum(wq.astype(F32) * knew, axis=1, keepdims=True)
        m = jnp.maximum(jnp.maximum(jnp.max(s, axis=1, keepdims=True), s_new), sink)
        e = jnp.exp(s - m)
        e_new = jnp.exp(s_new - m)
        denom = jnp.sum(e, axis=1, keepdims=True) + e_new + jnp.exp(sink - m)
        o = _nt((e / denom).astype(BF16), vt.astype(BF16)) + (e_new / denom) * vnew
        out = jnp.zeros((n_heads, HEAD_DIM), F32)
        for c in range(kvw // HEAD_DIM):
            out = jnp.where(head // group == c, o[:, c * HEAD_DIM:(c + 1) * HEAD_DIM], out)
        o_ref[i] = out
        mine = col_id == pl.program_id(0) * nb + i
        knew_col = jnp.sum(jnp.where(mine, knewt_ref[...], 0.0), axis=1, keepdims=True)
        vnew_col = jnp.sum(jnp.where(mine, vnewt_ref[...], 0.0), axis=1, keepdims=True)
        wk_ref[i] = jnp.where(pos == w - 1, knew_col, pltpu.roll(kt, w - 1, 1))
        wv_ref[i] = jnp.where(pos == w - 1, vnew_col, pltpu.roll(vt, w - 1, 1))


def _swa_sample(wq, state_kt, state_vt, knew, vnew, sinks):
    n, n_heads, kvw = wq.shape
    w = state_kt.shape[2]
    nb = 8
    blk = lambda shape: pl.BlockSpec((nb,) + shape, lambda i: (i,) + (0,) * len(shape))
    knew_t = knew.T
    vnew_t = vnew.T
    return pl.pallas_call(
        _swa_sample_kernel,
        grid=(n // nb,),
        in_specs=[blk((n_heads, kvw)), blk((kvw, w)), blk((kvw, w)), blk((1, kvw)), blk((1, kvw)),
                  _const_spec(knew_t.shape), _const_spec(vnew_t.shape), _const_spec(sinks.shape)],
        out_specs=[blk((n_heads, HEAD_DIM)), blk((kvw, w)), blk((kvw, w))],
        out_shape=[jax.ShapeDtypeStruct((n, n_heads, HEAD_DIM), F32),
                   jax.ShapeDtypeStruct((n, kvw, w), F32), jax.ShapeDtypeStruct((n, kvw, w), F32)],
        compiler_params=pltpu.CompilerParams(dimension_semantics=("parallel",),
                                             vmem_limit_bytes=VMEM_LIMIT),
        name="swa_sample",
    )(wq, state_kt, state_vt, knew[:, None, :], vnew[:, None, :], knew_t, vnew_t, sinks)


def _gated_out_kernel(o_ref, gate_ref, h_ref, wo_ref, y_ref):
    y_ref[...] = h_ref[...] + jnp.dot((o_ref[...] * _silu(gate_ref[...])).astype(BF16), wo_ref[...],
                                      preferred_element_type=F32)


def _gated_out(o, gate, h, wo):
    return pl.pallas_call(
        _gated_out_kernel,
        grid=(1,),
        in_specs=[_const_spec(a.shape) for a in (o, gate, h, wo)],
        out_specs=_const_spec(h.shape),
        out_shape=jax.ShapeDtypeStruct(h.shape, F32),
        compiler_params=pltpu.CompilerParams(vmem_limit_bytes=VMEM_LIMIT),
        name="gated_out",
    )(o, gate, h, wo)


def _rope_tables(pos):
    inv = 1.0 / (ROPE_THETA ** (jnp.arange(HALF, dtype=F32) / HALF))
    ang = pos.astype(F32)[:, None] * inv[None, :]
    c, s = jnp.cos(ang), jnp.sin(ang)
    return jnp.tile(c, (1, LANES // HALF)), jnp.tile(jnp.concatenate([-s, s], axis=1), (1, HEADS_PER_VREG))


def _slot_rows(q, n_heads, n_kv):
    n = q.shape[0]
    group = n_heads // n_kv
    onehot = (jnp.arange(n_heads)[:, None] // group == jnp.arange(n_kv)[None, :]).astype(q.dtype)
    q4 = q.reshape(n, n_heads, 1, HEAD_DIM) * onehot[None, :, :, None]
    return q4.reshape(n, n_heads, n_kv * HEAD_DIM)


def kernel(x_prompt, x_sample, cache_a_k, cache_a_v, state_b_k, state_b_v, page_table, g_a, w_in_a, qn_a, kn_a, w_out_a, g_kv, w_kv, kn_b, g_b, w_in_b, qn_b, sinks_b, w_out_b):
    _, s, d = x_prompt.shape
    n, t, _ = x_sample.shape
    assert t == 1 and g_a.shape[0] == 1 and g_b.shape[0] == 1
    n_heads = w_out_a.shape[1] // HEAD_DIM
    b_heads = w_out_b.shape[1] // HEAD_DIM
    past = page_table.shape[1] * PAGE_SIZE
    n_pool = cache_a_k.shape[1]
    a_kvw = A_KV_HEADS * HEAD_DIM
    b_kvw = B_KV_HEADS * HEAD_DIM

    cos_p, sin_p = _rope_tables(jnp.arange(s, dtype=jnp.int32))
    cos_s, sin_s = _rope_tables(jnp.full((n,), past, jnp.int32))
    lane = jnp.arange(LANES)
    b2 = (lane[:, None] // HEAD_DIM == lane[None, :] // HEAD_DIM).astype(BF16)
    gain_a = jnp.concatenate([jnp.tile(qn_a[0], n_heads), jnp.tile(kn_a[0], A_KV_HEADS)])[None, :]
    w_a = w_in_a[0].astype(BF16)
    wo_a = w_out_a[0].astype(BF16)
    w_kvb = w_kv.astype(BF16)
    w_b = w_in_b[0].astype(BF16)
    wo_b = w_out_b[0].astype(BF16)
    knb = jnp.tile(kn_b, HEADS_PER_VREG)[None, :]
    qnb = jnp.tile(qn_b[0], HEADS_PER_VREG)[None, :]
    post_consts = (wo_a, g_kv[None, :], w_kvb, knb, g_b[0][None, :], w_b, qnb)

    xp = x_prompt[0]
    qa, ka, va, k_p, v_p, gate_p = _proj_a_prompt(xp, w_a, g_a[0][None, :], gain_a, cos_p, sin_p, b2,
                                                  n_heads, A_KV_HEADS)
    o_p = _moba_prompt(qa, ka, va)
    h_p, kb_p, vb_p, qb_p, gateb_p = _post_a(o_p, gate_p, xp, *post_consts, cos_p, sin_p, b2, POST_A_ROWS)
    y_p = _swa_prompt(qb_p, kb_p, vb_p, sinks_b[0], gateb_p, h_p, wo_b)

    xs = x_sample[:, 0]
    q_s, k_s, v_s, gate_s = _proj_a_sample(xs, w_a, g_a[0][None, :], gain_a, cos_s, sin_s, b2,
                                           n_heads, A_KV_HEADS)
    feature_major = lambda a: jnp.moveaxis(a, -3, -1).reshape(a.shape[:-3] + (a.shape[-2] * HEAD_DIM, a.shape[-3]))
    o_s = _moba_sample(page_table, _slot_rows(q_s, n_heads, A_KV_HEADS), k_s[:, None, :], v_s[:, None, :],
                       feature_major(cache_a_k[0]), feature_major(cache_a_v[0]))
    h_s, kb_s, vb_s, qb_s, gateb_s = _post_a(o_s.reshape(n, n_heads * HEAD_DIM), gate_s, xs, *post_consts,
                                             cos_s, sin_s, b2, n)
    ob_s, win_skt, win_svt = _swa_sample(_slot_rows(qb_s, b_heads, B_KV_HEADS),
                                         feature_major(state_b_k), feature_major(state_b_v),
                                         kb_s, vb_s, sinks_b[0][:, None])
    y_s = _gated_out(ob_s.reshape(n, b_heads * HEAD_DIM), gateb_s, h_s, wo_b)
    token_major = lambda a: jnp.moveaxis(a.reshape(n, B_KV_HEADS, HEAD_DIM, WINDOW), -1, 1)
    win_sk, win_sv = token_major(win_skt), token_major(win_svt)

    return (y_p[None], y_s[:, None, :],
            k_p.reshape(1, 1, s, A_KV_HEADS, HEAD_DIM), v_p.reshape(1, 1, s, A_KV_HEADS, HEAD_DIM),
            kb_p[s - WINDOW:].reshape(1, WINDOW, B_KV_HEADS, HEAD_DIM),
            vb_p[s - WINDOW:].reshape(1, WINDOW, B_KV_HEADS, HEAD_DIM),
            k_s.reshape(1, n, 1, A_KV_HEADS, HEAD_DIM), v_s.reshape(1, n, 1, A_KV_HEADS, HEAD_DIM),
            win_sk, win_sv)
```
